```python
import jax, jax.numpy as jnp
from jax import lax
import numpy as np

D_MODEL = 1024
BATCH = 4
SEQ = 4096
DEPTH = 4

CHUNK = 64
Q_BLOCK = 128
N_MIXERS = 3
EPS = 1e-6

RET_HEADS = 4
RET_DK = D_MODEL
RET_DV = 2 * D_MODEL
RET_HK = RET_DK // RET_HEADS
RET_HV = RET_DV // RET_HEADS
RET_IN = 2 * RET_DK + 2 * RET_DV
ROPE_BASE = 10000.0

FOX_HEADS = 16
FOX_HD = 64
FOX_W = FOX_HEADS * FOX_HD
FOX_IN = 4 * FOX_W + FOX_HEADS
FOX_FORGET_BIAS = 2.0

GLA_HEADS = 4
GLA_DK = D_MODEL // 2
GLA_DV = D_MODEL
GLA_HK = GLA_DK // GLA_HEADS
GLA_HV = GLA_DV // GLA_HEADS
GLA_RANK = 16
GLA_TAU = 16.0
GLA_IN = 2 * GLA_DK + 2 * GLA_DV + GLA_RANK

MAX_POS_OFFSET = 100000

kernel_name = "hybrid_retention_fox_gla_adaln_trunk"


def _rms_norm(x, gain=None):
    xf = x.astype(jnp.float32)
    y = xf * lax.rsqrt(jnp.mean(xf * xf, axis=-1, keepdims=True) + EPS)
    if gain is not None:
        y = y * gain.astype(jnp.float32)
    return y.astype(x.dtype)


def _layer_norm(x):
    xf = x.astype(jnp.float32)
    mu = jnp.mean(xf, axis=-1, keepdims=True)
    var = jnp.mean(jnp.square(xf - mu), axis=-1, keepdims=True)
    return ((xf - mu) * lax.rsqrt(var + EPS)).astype(x.dtype)


def _rope(x, positions):
    half = x.shape[-1] // 2
    inv = ROPE_BASE ** (-jnp.arange(half, dtype=jnp.float32) / half)
    ang = positions.astype(jnp.float32)[..., None, None] * inv
    cos, sin = jnp.cos(ang), jnp.sin(ang)
    x1, x2 = x[..., :half], x[..., half:]
    return jnp.concatenate([x1 * cos - x2 * sin, x1 * sin + x2 * cos], axis=-1)


def _to_chunks(t):
    b, s, h, d = t.shape
    return t.reshape(b, s // CHUNK, CHUNK, h, d).transpose(0, 3, 1, 2, 4)


def _from_chunks(t):
    b, h, n, c, d = t.shape
    return t.transpose(0, 2, 3, 1, 4).reshape(b, n * c, h, d)


def _chunk_state_scan(q_in, k_in, v, state_decay):
    b, h, _, _, dk = q_in.shape
    dv = v.shape[-1]

    def step(state, inp):
        qi, ki, vi, di = inp
        out = jnp.einsum('bhcd,bhde->bhce', qi, state)
        state = di[..., None] * state + jnp.einsum('bhcd,bhce->bhde', ki, vi)
        return state, out

    xs = (jnp.moveaxis(q_in, 2, 0), jnp.moveaxis(k_in, 2, 0), jnp.moveaxis(v, 2, 0),
          jnp.moveaxis(state_decay, 2, 0))
    init = jnp.zeros((b, h, dk, dv), q_in.dtype)
    _, outs = lax.scan(step, init, xs)
    return jnp.moveaxis(outs, 0, 2)


def _retention(h, positions, w_in, w_out):
    b, s, _ = h.shape
    n = s // CHUNK
    q, k, v, g = jnp.split(h @ w_in, [RET_DK, 2 * RET_DK, 2 * RET_DK + RET_DV], axis=-1)
    q = _rope(q.astype(jnp.float32).reshape(b, s, RET_HEADS, RET_HK), positions) * (RET_HK ** -0.5)
    k = _rope(k.astype(jnp.float32).reshape(b, s, RET_HEADS, RET_HK), positions)
    qc, kc = _to_chunks(q), _to_chunks(k)
    vc = _to_chunks(v.astype(jnp.float32).reshape(b, s, RET_HEADS, RET_HV))
    log_gamma = jnp.log1p(-jnp.exp2(-5.0 - jnp.arange(RET_HEADS, dtype=jnp.float32)))
    pos = jnp.arange(CHUNK, dtype=jnp.float32)
    dist = jnp.abs(pos[:, None] - pos[None, :])
    intra_decay = jnp.exp(log_gamma[:, None, None] * dist)
    scores = jnp.einsum('bhncd,bhnmd->bhncm', qc, kc) * intra_decay[:, None]
    o = jnp.einsum('bhncm,bhnme->bhnce', scores, vc)
    q_in = qc * jnp.exp(log_gamma[:, None] * (pos + 1.0))[:, None, :, None]
    k_in = kc * jnp.exp(log_gamma[:, None] * (CHUNK - 1.0 - pos))[:, None, :, None]
    state_decay = jnp.broadcast_to(jnp.exp(log_gamma * CHUNK)[None, :, None, None], (b, RET_HEADS, n, RET_HK))
    o = o + _chunk_state_scan(q_in, k_in, vc, state_decay)
    o = _layer_norm(_from_chunks(o)).reshape(b, s, RET_DV)
    return (o.astype(h.dtype) * jax.nn.silu(g)) @ w_out


def _forgetting_attention(h, w_in, b_f, q_gain, k_gain, w_out):
    b, s, _ = h.shape
    q, k, v, g, f = jnp.split(h @ w_in, [FOX_W, 2 * FOX_W, 3 * FOX_W, 4 * FOX_W], axis=-1)
    q = _rms_norm(q.astype(jnp.float32).reshape(b, s, FOX_HEADS, FOX_HD), q_gain).transpose(0, 2, 1, 3) * (FOX_HD ** -0.5)
    k = _rms_norm(k.astype(jnp.float32).reshape(b, s, FOX_HEADS, FOX_HD), k_gain).transpose(0, 2, 1, 3)
    v = v.astype(jnp.float32).reshape(b, s, FOX_HEADS, FOX_HD).transpose(0, 2, 1, 3)
    log_f = jax.nn.log_sigmoid(f.astype(jnp.float32) + b_f.astype(jnp.float32))
    cum = jnp.cumsum(log_f, axis=1).transpose(0, 2, 1)
    outs = []
    for qs in range(0, s, Q_BLOCK):
        qe = qs + Q_BLOCK
        logits = (jnp.einsum('bhqd,bhkd->bhqk', q[:, :, qs:qe], k[:, :, :qe])
                  + cum[:, :, qs:qe, None] - cum[:, :, None, :qe])
        causal = jnp.arange(qs, qe)[:, None] >= jnp.arange(qe)[None, :]
        p = jax.nn.softmax(jnp.where(causal, logits, -jnp.inf), axis=-1)
        outs.append(jnp.einsum('bhqk,bhkd->bhqd', p, v[:, :, :qe]))
    o = jnp.concatenate(outs, axis=2).transpose(0, 2, 1, 3).reshape(b, s, FOX_W)
    return (o.astype(h.dtype) * jax.nn.silu(g)) @ w_out


def _gated_linear_attention(h, w_in, w_gate2, b_gate, w_out):
    b, s, _ = h.shape
    q, k, v, g, r = jnp.split(h @ w_in, [GLA_DK, 2 * GLA_DK, 2 * GLA_DK + GLA_DV, 2 * GLA_DK + 2 * GLA_DV], axis=-1)
    log_a = jax.nn.log_sigmoid((r @ w_gate2 + b_gate).astype(jnp.float32)) / GLA_TAU
    qc = _to_chunks(q.astype(jnp.float32).reshape(b, s, GLA_HEADS, GLA_HK)) * (GLA_HK ** -0.5)
    kc = _to_chunks(k.astype(jnp.float32).reshape(b, s, GLA_HEADS, GLA_HK))
    vc = _to_chunks(v.astype(jnp.float32).reshape(b, s, GLA_HEADS, GLA_HV))
    cb = jnp.cumsum(_to_chunks(log_a.reshape(b, s, GLA_HEADS, GLA_HK)), axis=3)
    cb_last = cb[:, :, :, -1:, :]
    eb, enb = jnp.exp(cb), jnp.exp(-cb)
    a_causal = jnp.einsum('bhncd,bhnmd->bhncm', qc * eb, kc * enb)
    a_anti = jnp.einsum('bhncd,bhnmd->bhncm', qc * enb, kc * eb)
    idx = jnp.arange(CHUNK)
    attn = jnp.where(idx[:, None] >= idx[None, :], a_causal, a_anti)
    o = jnp.einsum('bhncm,bhnme->bhnce', attn, vc)
    o = o + _chunk_state_scan(qc * eb, kc * jnp.exp(cb_last - cb), vc, jnp.exp(cb_last[:, :, :, 0, :]))
    o = _rms_norm(_from_chunks(o)).reshape(b, s, GLA_DV)
    return (o.astype(h.dtype) * jax.nn.silu(g)) @ w_out


def setup_inputs(seed: int = 0) -> dict:
    key = jax.random.key(seed)
    ks = jax.random.split(key, 20)
    f32 = jnp.float32
    n_a = len(range(0, DEPTH, N_MIXERS))
    n_b = len(range(1, DEPTH, N_MIXERS))
    n_c = len(range(2, DEPTH, N_MIXERS))

    def w(k, shape, fan_in):
        return jax.random.normal(k, shape, f32) * (fan_in ** -0.5)

    x = jax.random.normal(ks[0], (BATCH, SEQ, D_MODEL), f32)
    c = jax.random.normal(ks[1], (BATCH, D_MODEL), f32)
    offs = jax.random.randint(ks[2], (BATCH, 1), 0, MAX_POS_OFFSET, dtype=jnp.int32)
    positions = (offs + jnp.arange(SEQ, dtype=jnp.int32)[None, :]).astype(jnp.int32)
    return {
        "x": x,
        "c": c,
        "positions": positions,
        "mod_w": w(ks[3], (DEPTH, D_MODEL, 3 * D_MODEL), D_MODEL),
        "mod_b": 0.01 * jax.random.normal(ks[4], (DEPTH, 3 * D_MODEL), f32),
        "norm_g": 1.0 + 0.01 * jax.random.normal(ks[5], (DEPTH, D_MODEL), f32),
        "ret_w_in": w(ks[6], (n_a, D_MODEL, RET_IN), D_MODEL),
        "ret_w_out": w(ks[7], (n_a, RET_DV, D_MODEL), RET_DV),
        "fox_w_in": w(ks[8], (n_b, D_MODEL, FOX_IN), D_MODEL),
        "fox_b_f": FOX_FORGET_BIAS + 0.1 * jax.random.normal(ks[9], (n_b, FOX_HEADS), f32),
        "fox_q_gain": 1.0 + 0.01 * jax.random.normal(ks[10], (n_b, FOX_HD), f32),
        "fox_k_gain": 1.0 + 0.01 * jax.random.normal(ks[11], (n_b, FOX_HD), f32),
        "fox_w_out": w(ks[12], (n_b, FOX_W, D_MODEL), FOX_W),
        "gla_w_in": w(ks[13], (n_c, D_MODEL, GLA_IN), D_MODEL),
        "gla_w_gate2": w(ks[14], (n_c, GLA_RANK, GLA_DK), GLA_RANK),
        "gla_b_gate": 0.01 * jax.random.normal(ks[15], (n_c, GLA_DK), f32),
        "gla_w_out": w(ks[16], (n_c, GLA_DV, D_MODEL), GLA_DV),
        "final_g": 1.0 + 0.01 * jax.random.normal(ks[17], (D_MODEL,), f32),
    }


def reference(x, c, positions, mod_w, mod_b, norm_g, ret_w_in, ret_w_out, fox_w_in, fox_b_f,
              fox_q_gain, fox_k_gain, fox_w_out, gla_w_in, gla_w_gate2, gla_b_gate, gla_w_out, final_g):
    c_act = jax.nn.silu(c)
    for i in range(DEPTH):
        shift, scale, gate = jnp.split(c_act @ mod_w[i] + mod_b[i], 3, axis=-1)
        h = _rms_norm(x, norm_g[i]) * (1.0 + scale[:, None, :]) + shift[:, None, :]
        j = i // N_MIXERS
        kind = i % N_MIXERS
        if kind == 0:
            y = _retention(h, positions, ret_w_in[j], ret_w_out[j])
        elif kind == 1:
            y = _forgetting_attention(h, fox_w_in[j], fox_b_f[j], fox_q_gain[j], fox_k_gain[j], fox_w_out[j])
        else:
            y = _gated_linear_attention(h, gla_w_in[j], gla_w_gate2[j], gla_b_gate[j], gla_w_out[j])
        x = x + gate[:, None, :] * y
    return _rms_norm(x, final_g)
```

```python
import functools
import math

import jax
import jax.numpy as jnp
from jax import lax
from jax.experimental import pallas as pl
from jax.experimental.pallas import tpu as pltpu

F32 = jnp.float32
BF16 = jnp.bfloat16

EPS = 1e-6
N_MIXERS = 3
CHUNK = 64
RET_HEADS = 4
ROPE_BASE = 10000.0
FOX_HEADS = 16
GLA_HEADS = 4
GLA_RANK = 16
GLA_TAU = 16.0

LANES = 128
RET_SUPER = 256
NEG_BIG = -1e30
VMEM_LIMIT = 56 * 1024 * 1024

NT_DIMS = (((1,), (1,)), ((), ()))
TN_DIMS = (((0,), (0,)), ((), ()))


def _silu(x):
    return x * (1.0 / (1.0 + jnp.exp(-x)))


def _log_sigmoid(z):
    return jnp.minimum(z, 0.0) - jnp.log1p(jnp.exp(-jnp.abs(z)))


def _dot(a, b):
    return jnp.dot(a, b, preferred_element_type=F32)


def _split3(x):
    hi = x.astype(BF16)
    r1 = x - hi.astype(F32)
    mid = r1.astype(BF16)
    lo = (r1 - mid.astype(F32)).astype(BF16)
    return hi, mid, lo


def _tri_cumsum(tri, x):
    hi, mid, lo = _split3(x)
    return _dot(tri, hi) + _dot(tri, mid) + _dot(tri, lo)


def _lower_tri(n):
    row = lax.broadcasted_iota(jnp.int32, (n, n), 0)
    col = lax.broadcasted_iota(jnp.int32, (n, n), 1)
    return row >= col


def _norm_mod(x_ref, ng_ref, sc_ref, sh_ref):
    x = x_ref[...]
    ms = jnp.mean(x * x, axis=-1, keepdims=True)
    y = x * lax.rsqrt(ms + EPS) * ng_ref[...]
    return (y * (1.0 + sc_ref[...]) + sh_ref[...]).astype(BF16)


def _project(h, w_ref, col0, width, o_ref, tn):
    for n0 in range(0, width, tn):
        o_ref[:, n0:n0 + tn] = _dot(h, w_ref[:, col0 + n0:col0 + n0 + tn]).astype(o_ref.dtype)


def _params(*sem):
    return pltpu.CompilerParams(dimension_semantics=sem, vmem_limit_bytes=VMEM_LIMIT)


def _resident(shape):
    nd = len(shape)
    return pl.BlockSpec(shape, lambda *_: (0,) * nd, pipeline_mode=pl.Buffered(1))


def _mod_kernel(c_ref, w_ref, b_ref, o_ref):
    a = _silu(c_ref[...])
    o_ref[...] = jnp.dot(a, w_ref[...], preferred_element_type=F32,
                         precision=lax.Precision.HIGHEST) + b_ref[...]


def _modulation(c, mod_w, mod_b, tn=1024):
    depth, d, d3 = mod_w.shape
    b = c.shape[0]
    return pl.pallas_call(
        _mod_kernel,
        grid=(depth, d3 // tn),
        in_specs=[
            pl.BlockSpec((b, d), lambda i, n: (0, 0)),
            pl.BlockSpec((None, d, tn), lambda i, n: (i, 0, n)),
            pl.BlockSpec((None, 1, tn), lambda i, n: (i, 0, n)),
        ],
        out_specs=pl.BlockSpec((None, b, tn), lambda i, n: (i, 0, n)),
        out_shape=jax.ShapeDtypeStruct((depth, b, d3), F32),
        compiler_params=_params("parallel", "parallel"),
        name="adaln_modulation",
    )(c, mod_w, mod_b.reshape(depth, 1, d3))


def _layer_vec_specs(layer, d):
    ng = pl.BlockSpec((None, 1, d), lambda b, m: (layer, 0, 0))

    def mod(which):
        return pl.BlockSpec((None, None, None, 1, d), lambda b, m: (layer, b, which, 0, 0))
    return ng, mod


def _ret_in_kernel(x_ref, ng_ref, sc_ref, sh_ref, pos_ref, inv_ref, w_ref,
                   q_ref, k_ref, v_ref, g_ref, *, heads, hk, dv, tn):
    h = _norm_mod(x_ref, ng_ref, sc_ref, sh_ref)
    ang = pos_ref[...].astype(F32) * inv_ref[...]
    cos = jnp.cos(ang)
    sin = jnp.sin(ang)
    half = hk // 2
    dk = heads * hk
    for col0, o_ref, scale in ((0, q_ref, hk ** -0.5), (dk, k_ref, None)):
        for hd in range(heads):
            r = _dot(h, w_ref[:, col0 + hd * hk:col0 + (hd + 1) * hk])
            x1 = r[:, :half]
            x2 = r[:, half:]
            o1 = x1 * cos - x2 * sin
            o2 = x1 * sin + x2 * cos
            if scale is not None:
                o1 = o1 * scale
                o2 = o2 * scale
            o_ref[:, hd * hk:hd * hk + half] = o1.astype(BF16)
            o_ref[:, hd * hk + half:(hd + 1) * hk] = o2.astype(BF16)
    _project(h, w_ref, 2 * dk, dv, v_ref, tn)
    _project(h, w_ref, 2 * dk + dv, dv, g_ref, tn)


def _ret_in_proj(x, positions, norm_g, mod, layer, w_in, tm=512, tn=512):
    b, s, d = x.shape
    heads = RET_HEADS
    dk = d
    dv = 2 * d
    hk = dk // heads
    n_in = w_in.shape[1]
    half = hk // 2
    inv = (ROPE_BASE ** (-jnp.arange(half, dtype=F32) / half)).reshape(1, half)
    ng, modspec = _layer_vec_specs(layer, d)
    tok = lambda w: pl.BlockSpec((None, tm, w), lambda bi, m: (bi, m, 0))
    return pl.pallas_call(
        functools.partial(_ret_in_kernel, heads=heads, hk=hk, dv=dv, tn=tn),
        grid=(b, s // tm),
        in_specs=[tok(d), ng, modspec(1), modspec(0), tok(1), _resident((1, half)), _resident((d, n_in))],
        out_specs=[tok(dk), tok(dk), tok(dv), tok(dv)],
        out_shape=[jax.ShapeDtypeStruct((b, s, w), BF16) for w in (dk, dk, dv, dv)],
        compiler_params=_params("parallel", "parallel"),
        name="retention_in_proj",
    )(x, norm_g, mod, mod, positions.reshape(b, s, 1), inv, w_in)


def _ret_core_kernel(q_ref, k_ref, v_ref, g_ref, y_ref, state_ref, dmask_ref, *, heads, hk, hv, sup):
    log_gamma = [math.log1p(-(2.0 ** (-5.0 - h))) for h in range(heads)]

    @pl.when(pl.program_id(1) == 0)
    def _():
        state_ref[...] = jnp.zeros_like(state_ref)
        t = lax.broadcasted_iota(jnp.int32, (sup, sup), 0)
        s = lax.broadcasted_iota(jnp.int32, (sup, sup), 1)
        dist = jnp.abs(t - s).astype(F32)
        visible = (s // CHUNK) <= (t // CHUNK)
        for h in range(heads):
            dmask_ref[h] = jnp.where(visible, jnp.exp(log_gamma[h] * dist), 0.0)

    pos = lax.broadcasted_iota(jnp.int32, (sup, 1), 0).astype(F32)
    for h in range(heads):
        q = q_ref[:, h * hk:(h + 1) * hk]
        k = k_ref[:, h * hk:(h + 1) * hk]
        v = v_ref[:, h * hv:(h + 1) * hv]
        scores = lax.dot_general(q, k, NT_DIMS, preferred_element_type=F32) * dmask_ref[h]
        o = _dot(scores.astype(BF16), v)
        state = state_ref[h]
        q_decay = jnp.exp(log_gamma[h] * (pos + 1.0))
        o = o + q_decay * _dot(q, state.astype(BF16))
        k_decay = jnp.exp(log_gamma[h] * (sup - 1.0 - pos))
        k_in = (k.astype(F32) * k_decay).astype(BF16)
        state_ref[h] = math.exp(log_gamma[h] * sup) * state + lax.dot_general(
            k_in, v, TN_DIMS, preferred_element_type=F32)
        mu = jnp.mean(o, axis=-1, keepdims=True)
        oc = o - mu
        var = jnp.mean(oc * oc, axis=-1, keepdims=True)
        gate = _silu(g_ref[:, h * hv:(h + 1) * hv].astype(F32))
        y_ref[:, h * hv:(h + 1) * hv] = (oc * lax.rsqrt(var + EPS) * gate).astype(BF16)


def _ret_core(q, k, v, g):
    b, s, dk = q.shape
    dv = v.shape[-1]
    heads = RET_HEADS
    hk, hv = dk // heads, dv // heads
    sup = RET_SUPER
    tok = lambda w: pl.BlockSpec((None, sup, w), lambda bi, l: (bi, l, 0))
    return pl.pallas_call(
        functools.partial(_ret_core_kernel, heads=heads, hk=hk, hv=hv, sup=sup),
        grid=(b, s // sup),
        in_specs=[tok(dk), tok(dk), tok(dv), tok(dv)],
        out_specs=tok(dv),
        out_shape=jax.ShapeDtypeStruct((b, s, dv), BF16),
        scratch_shapes=[pltpu.VMEM((heads, hk, hv), F32), pltpu.VMEM((heads, sup, sup), F32)],
        compiler_params=_params("parallel", "arbitrary"),
        name="retention_core",
    )(q, k, v, g)


def _fox_in_kernel(x_ref, ng_ref, sc_ref, sh_ref, w_ref, wf_ref, bf_ref,
                   q_ref, k_ref, v_ref, g_ref, cum_ref, cumt_ref, carry_ref, *, width, tn):
    @pl.when(pl.program_id(1) == 0)
    def _():
        carry_ref[...] = jnp.zeros_like(carry_ref)

    h = _norm_mod(x_ref, ng_ref, sc_ref, sh_ref)
    for idx, o_ref in enumerate((q_ref, k_ref, v_ref, g_ref)):
        _project(h, w_ref, idx * width, width, o_ref, tn)
    log_f = _log_sigmoid(_dot(h, wf_ref[...]) + bf_ref[...])
    rows = log_f.shape[0]
    tri = jnp.where(_lower_tri(rows), 1.0, 0.0).astype(BF16)
    cum = _tri_cumsum(tri, log_f) + carry_ref[...]
    carry_ref[...] = cum[rows - 1:rows, :]
    cum_ref[...] = cum
    cumt_ref[...] = cum.T


def _fox_in_proj(x, norm_g, mod, layer, w_in, b_f, tm=512, tn=512):
    b, s, d = x.shape
    heads = FOX_HEADS
    width = (w_in.shape[1] - heads) // 4
    w_main = w_in[:, :4 * width]
    w_f = jnp.pad(w_in[:, 4 * width:], ((0, 0), (0, LANES - heads)))
    bias = jnp.pad(b_f.astype(F32), (0, LANES - heads)).reshape(1, LANES)
    ng, modspec = _layer_vec_specs(layer, d)
    tok = lambda w: pl.BlockSpec((None, tm, w), lambda bi, m: (bi, m, 0))
    return pl.pallas_call(
        functools.partial(_fox_in_kernel, width=width, tn=tn),
        grid=(b, s // tm),
        in_specs=[tok(d), ng, modspec(1), modspec(0),
                  _resident((d, 4 * width)), _resident((d, LANES)), _resident((1, LANES))],
        out_specs=[tok(width)] * 4 + [tok(LANES), pl.BlockSpec((None, LANES, tm), lambda bi, m: (bi, 0, m))],
        out_shape=[jax.ShapeDtypeStruct((b, s, width), BF16)] * 4
                  + [jax.ShapeDtypeStruct((b, s, LANES), F32), jax.ShapeDtypeStruct((b, LANES, s), F32)],
        scratch_shapes=[pltpu.VMEM((1, LANES), F32)],
        compiler_params=_params("parallel", "arbitrary"),
        name="fox_in_proj",
    )(x, norm_g, mod, mod, w_main, w_f, bias)


def _fox_flash_kernel(q_ref, k_ref, v_ref, g_ref, cumt_ref, cumc_ref, qg_ref, kg_ref, y_ref,
                      kn_ref, m_ref, l_ref, acc_ref, *, bq, hd, seq):
    pair = pl.program_id(1)
    qi = pl.program_id(2)
    lane = lax.broadcasted_iota(jnp.int32, (1, LANES), 1)
    first = lane < hd

    def head_norm(x, gain):
        sq = x * x
        s0 = jnp.sum(jnp.where(first, sq, 0.0), axis=-1, keepdims=True)
        s1 = jnp.sum(jnp.where(first, 0.0, sq), axis=-1, keepdims=True)
        ms = jnp.where(first, s0, s1) * (1.0 / hd)
        return x * lax.rsqrt(ms + EPS) * gain

    @pl.when(qi == 0)
    def _():
        for r0 in range(0, seq, bq):
            kn_ref[r0:r0 + bq, :] = head_norm(k_ref[r0:r0 + bq, :].astype(F32), kg_ref[...]).astype(BF16)

    qn = head_norm(q_ref[...].astype(F32), qg_ref[...]) * (hd ** -0.5)
    cumc = cumc_ref[...]
    causal = _lower_tri(bq)
    outs = []
    for j in range(2):
        sel = first if j == 0 else jnp.logical_not(first)
        qm = jnp.where(sel, qn, 0.0).astype(BF16)
        cq = jnp.sum(jnp.where(lane == 2 * pair + j, cumc, 0.0), axis=-1, keepdims=True)
        m_ref[...] = jnp.full_like(m_ref, NEG_BIG)
        l_ref[...] = jnp.zeros_like(l_ref)
        acc_ref[...] = jnp.zeros_like(acc_ref)

        def step(kb, masked):
            r0 = pl.multiple_of(kb * bq, bq)
            s = lax.dot_general(qm, kn_ref[pl.ds(r0, bq), :], NT_DIMS, preferred_element_type=F32)
            u = s - cumt_ref[j:j + 1, pl.ds(r0, bq)]
            if masked:
                u = jnp.where(causal, u, NEG_BIG)
            m_prev = m_ref[...]
            m_new = jnp.maximum(m_prev, jnp.max(u, axis=-1, keepdims=True) + cq)
            p = jnp.exp(u + (cq - m_new))
            alpha = jnp.exp(m_prev - m_new)
            l_ref[...] = alpha * l_ref[...] + jnp.sum(p, axis=-1, keepdims=True)
            acc_ref[...] = alpha * acc_ref[...] + _dot(p.astype(BF16), v_ref[pl.ds(r0, bq), :])
            m_ref[...] = m_new

        def body(kb, carry):
            step(kb, False)
            return carry

        lax.fori_loop(0, qi, body, 0)
        step(qi, True)
        outs.append(acc_ref[...] / l_ref[...])
    o = jnp.where(first, outs[0], outs[1])
    y_ref[...] = (o * _silu(g_ref[...].astype(F32))).astype(BF16)


def _fox_flash(q, k, v, g, cum, cumt, q_gain, k_gain, bq=512):
    b, s, width = q.shape
    heads = FOX_HEADS
    hd = width // heads
    assert 2 * hd == LANES
    pairs = heads // 2
    cumt_pairs = cumt[:, :heads, :].reshape(b, pairs, 2, s)
    qg = jnp.tile(q_gain.astype(F32), 2).reshape(1, LANES)
    kg = jnp.tile(k_gain.astype(F32), 2).reshape(1, LANES)
    blk = pl.BlockSpec((None, bq, LANES), lambda bi, p, i: (bi, i, p))
    full = pl.BlockSpec((None, s, LANES), lambda bi, p, i: (bi, 0, p))
    gain = pl.BlockSpec((1, LANES), lambda bi, p, i: (0, 0))
    return pl.pallas_call(
        functools.partial(_fox_flash_kernel, bq=bq, hd=hd, seq=s),
        grid=(b, pairs, s // bq),
        in_specs=[blk, full, full, blk,
                  pl.BlockSpec((None, None, 2, s), lambda bi, p, i: (bi, p, 0, 0)),
                  pl.BlockSpec((None, bq, LANES), lambda bi, p, i: (bi, i, 0)),
                  gain, gain],
        out_specs=blk,
        out_shape=jax.ShapeDtypeStruct((b, s, width), BF16),
        scratch_shapes=[pltpu.VMEM((s, LANES), BF16), pltpu.VMEM((bq, 1), F32),
                        pltpu.VMEM((bq, 1), F32), pltpu.VMEM((bq, LANES), F32)],
        compiler_params=_params("parallel", "parallel", "arbitrary"),
        name="fox_flash_attention",
    )(q, k, v, g, cumt_pairs, cum, qg, kg)


def _gla_in_kernel(x_ref, ng_ref, sc_ref, sh_ref, w_ref, wr_ref, wg2_ref, bg_ref,
                   q_ref, k_ref, v_ref, g_ref, la_ref, *, dk, dv, tn):
    h = _norm_mod(x_ref, ng_ref, sc_ref, sh_ref)
    _project(h, w_ref, 0, dk, q_ref, tn)
    _project(h, w_ref, dk, dk, k_ref, tn)
    _project(h, w_ref, 2 * dk, dv, v_ref, tn)
    _project(h, w_ref, 2 * dk + dv, dv, g_ref, tn)
    r = _dot(h, wr_ref[...])
    z = _dot(r.astype(BF16), wg2_ref[...]) + bg_ref[...]
    la_ref[...] = _log_sigmoid(z) * (1.0 / GLA_TAU)


def _gla_in_proj(x, norm_g, mod, layer, w_in, w_gate2, b_gate, tm=512, tn=512):
    b, s, d = x.shape
    dk, dv, rank = d // 2, d, GLA_RANK
    main = 2 * dk + 2 * dv
    w_main = w_in[:, :main]
    w_r = jnp.pad(w_in[:, main:], ((0, 0), (0, LANES - rank)))
    w_g2 = jnp.pad(w_gate2, ((0, LANES - rank), (0, 0)))
    ng, modspec = _layer_vec_specs(layer, d)
    tok = lambda w, : pl.BlockSpec((None, tm, w), lambda bi, m: (bi, m, 0))
    return pl.pallas_call(
        functools.partial(_gla_in_kernel, dk=dk, dv=dv, tn=tn),
        grid=(b, s // tm),
        in_specs=[tok(d), ng, modspec(1), modspec(0), _resident((d, main)), _resident((d, LANES)),
                  _resident((LANES, dk)), _resident((1, dk))],
        out_specs=[tok(dk), tok(dk), tok(dv), tok(dv), tok(dk)],
        out_shape=[jax.ShapeDtypeStruct((b, s, w), BF16) for w in (dk, dk, dv, dv)]
                  + [jax.ShapeDtypeStruct((b, s, dk), F32)],
        compiler_params=_params("parallel", "parallel"),
        name="gla_in_proj",
    )(x, norm_g, mod, mod, w_main, w_r, w_g2, b_gate.astype(F32).reshape(1, dk))


def _gla_core_kernel(q_ref, k_ref, v_ref, g_ref, la_ref, y_ref, state_ref, *, heads, hk, hv, nchunk):
    @pl.when(pl.program_id(1) == 0)
    def _():
        state_ref[...] = jnp.zeros_like(state_ref)

    causal = _lower_tri(CHUNK)
    tri = jnp.where(causal, 1.0, 0.0).astype(BF16)
    scale = hk ** -0.5

    def chunk(c, carry):
        r0 = pl.multiple_of(c * CHUNK, CHUNK)
        rows = pl.ds(r0, CHUNK)
        cb = _tri_cumsum(tri, la_ref[rows, :])
        cb_last = cb[CHUNK - 1:CHUNK, :]
        eb = jnp.exp(cb)
        enb = jnp.exp(-cb)
        qf = q_ref[rows, :].astype(F32) * scale
        kf = k_ref[rows, :].astype(F32)
        q_e = (qf * eb).astype(BF16)
        q_n = (qf * enb).astype(BF16)
        k_n = (kf * enb).astype(BF16)
        k_e = (kf * eb).astype(BF16)
        k_in = (kf * jnp.exp(cb_last - cb)).astype(BF16)
        state_decay = jnp.exp(cb_last)
        for h in range(heads):
            ks = slice(h * hk, (h + 1) * hk)
            vs = slice(h * hv, (h + 1) * hv)
            a_causal = lax.dot_general(q_e[:, ks], k_n[:, ks], NT_DIMS, preferred_element_type=F32)
            a_anti = lax.dot_general(q_n[:, ks], k_e[:, ks], NT_DIMS, preferred_element_type=F32)
            attn = jnp.where(causal, a_causal, a_anti).astype(BF16)
            v = v_ref[rows, vs]
            state_t = state_ref[h]
            o = _dot(attn, v) + lax.dot_general(q_e[:, ks], state_t.astype(BF16), NT_DIMS,
                                                preferred_element_type=F32)
            state_ref[h] = state_t * state_decay[:, ks] + lax.dot_general(
                v, k_in[:, ks], TN_DIMS, preferred_element_type=F32)
            ms = jnp.mean(o * o, axis=-1, keepdims=True)
            gate = _silu(g_ref[rows, vs].astype(F32))
            y_ref[rows, vs] = (o * lax.rsqrt(ms + EPS) * gate).astype(BF16)
        return carry

    lax.fori_loop(0, nchunk, chunk, 0)


def _gla_core(q, k, v, g, la, tl=512):
    b, s, dk = q.shape
    dv = v.shape[-1]
    heads = GLA_HEADS
    hk, hv = dk // heads, dv // heads
    tok = lambda w: pl.BlockSpec((None, tl, w), lambda bi, l: (bi, l, 0))
    return pl.pallas_call(
        functools.partial(_gla_core_kernel, heads=heads, hk=hk, hv=hv, nchunk=tl // CHUNK),
        grid=(b, s // tl),
        in_specs=[tok(dk), tok(dk), tok(dv), tok(dv), tok(dk)],
        out_specs=tok(dv),
        out_shape=jax.ShapeDtypeStruct((b, s, dv), BF16),
        scratch_shapes=[pltpu.VMEM((heads, hv, hk), F32)],
        compiler_params=_params("parallel", "arbitrary"),
        name="gla_core",
    )(q, k, v, g, la)


def _out_kernel(y_ref, w_ref, x_ref, gate_ref, fg_ref, o_ref, *, final):
    xn = x_ref[...] + gate_ref[...] * _dot(y_ref[...], w_ref[...])
    if final:
        ms = jnp.mean(xn * xn, axis=-1, keepdims=True)
        xn = xn * lax.rsqrt(ms + EPS) * fg_ref[...]
    o_ref[...] = xn


def _out_proj(y, w_out, x, mod, layer, final_g, final, tm=512):
    b, s, d = x.shape
    kdim = y.shape[-1]
    _, modspec = _layer_vec_specs(layer, d)
    tok = lambda w: pl.BlockSpec((None, tm, w), lambda bi, m: (bi, m, 0))
    return pl.pallas_call(
        functools.partial(_out_kernel, final=final),
        grid=(b, s // tm),
        in_specs=[tok(kdim), _resident((kdim, d)), tok(d), modspec(2), _resident((1, d))],
        out_specs=tok(d),
        out_shape=jax.ShapeDtypeStruct((b, s, d), F32),
        compiler_params=_params("parallel", "parallel"),
        name="out_proj_residual",
    )(y, w_out, x, mod, final_g)


def kernel(x, c, positions, mod_w, mod_b, norm_g, ret_w_in, ret_w_out, fox_w_in, fox_b_f, fox_q_gain,
           fox_k_gain, fox_w_out, gla_w_in, gla_w_gate2, gla_b_gate, gla_w_out, final_g):
    depth, d, _ = mod_w.shape
    b = x.shape[0]
    mod = _modulation(c, mod_w, mod_b).reshape(depth, b, 3, 1, d)
    ng = norm_g.astype(F32).reshape(depth, 1, d)
    fg = final_g.astype(F32).reshape(1, d)
    for i in range(depth):
        j = i // N_MIXERS
        kind = i % N_MIXERS
        if kind == 0:
            q, k, v, g = _ret_in_proj(x, positions, ng, mod, i, ret_w_in[j].astype(BF16))
            y = _ret_core(q, k, v, g)
            w_out = ret_w_out[j]
        elif kind == 1:
            q, k, v, g, cum, cumt = _fox_in_proj(x, ng, mod, i, fox_w_in[j].astype(BF16), fox_b_f[j])
            y = _fox_flash(q, k, v, g, cum, cumt, fox_q_gain[j], fox_k_gain[j])
            w_out = fox_w_out[j]
        else:
            q, k, v, g, la = _gla_in_proj(x, ng, mod, i, gla_w_in[j].astype(BF16),
                                          gla_w_gate2[j].astype(BF16), gla_b_gate[j])
            y = _gla_core(q, k, v, g, la)
            w_out = gla_w_out[j]
        x = _out_proj(y, w_out.astype(BF16), x, mod, i, fg, final=(i == depth - 1))
    return x
```

```python
import functools
import math

import jax
import jax.numpy as jnp
from jax import lax
from jax.experimental import pallas as pl
from jax.experimental.pallas import tpu as pltpu

F32 = jnp.float32
BF16 = jnp.bfloat16

EPS = 1e-6
N_MIXERS = 3
CHUNK = 64
RET_HEADS = 4
ROPE_BASE = 10000.0
FOX_HEADS = 16
GLA_HEADS = 4
GLA_RANK = 16
GLA_TAU = 16.0

LANES = 128
RET_SUPER = 256
NEG_BIG = -1e30
LOG2E = math.log2(math.e)
VMEM_LIMIT = 56 * 1024 * 1024

NT_DIMS = (((1,), (1,)), ((), ()))
TN_DIMS = (((0,), (0,)), ((), ()))


def _silu(x):
    return x * (1.0 / (1.0 + jnp.exp(-x)))


def _log_sigmoid(z):
    return jnp.minimum(z, 0.0) - jnp.log1p(jnp.exp(-jnp.abs(z)))


def _dot(a, b):
    return jnp.dot(a, b, preferred_element_type=F32)


def _split3(x):
    hi = x.astype(BF16)
    r1 = x - hi.astype(F32)
    mid = r1.astype(BF16)
    lo = (r1 - mid.astype(F32)).astype(BF16)
    return hi, mid, lo


def _tri_cumsum(tri, x):
    hi, mid, lo = _split3(x)
    return _dot(tri, hi) + _dot(tri, mid) + _dot(tri, lo)


def _lower_tri(n):
    row = lax.broadcasted_iota(jnp.int32, (n, n), 0)
    col = lax.broadcasted_iota(jnp.int32, (n, n), 1)
    return row >= col


def _norm_mod(x_ref, ng_ref, sc_ref, sh_ref):
    x = x_ref[...]
    ms = jnp.mean(x * x, axis=-1, keepdims=True)
    y = x * lax.rsqrt(ms + EPS) * ng_ref[...]
    return (y * (1.0 + sc_ref[...]) + sh_ref[...]).astype(BF16)


def _project(h, w_ref, col0, width, o_ref, tn):
    for n0 in range(0, width, tn):
        o_ref[:, n0:n0 + tn] = _dot(h, w_ref[:, col0 + n0:col0 + n0 + tn]).astype(o_ref.dtype)


def _params(*sem):
    return pltpu.CompilerParams(dimension_semantics=sem, vmem_limit_bytes=VMEM_LIMIT)


def _resident(shape):
    nd = len(shape)
    return pl.BlockSpec(shape, lambda *_: (0,) * nd, pipeline_mode=pl.Buffered(1))


def _mod_kernel(c_ref, w_ref, b_ref, o_ref):
    a = _silu(c_ref[...])
    o_ref[...] = jnp.dot(a, w_ref[...], preferred_element_type=F32,
                         precision=lax.Precision.HIGHEST) + b_ref[...]


def _modulation(c, mod_w, mod_b, tn=1024):
    depth, d, d3 = mod_w.shape
    b = c.shape[0]
    return pl.pallas_call(
        _mod_kernel,
        grid=(depth, d3 // tn),
        in_specs=[
            pl.BlockSpec((b, d), lambda i, n: (0, 0)),
            pl.BlockSpec((None, d, tn), lambda i, n: (i, 0, n)),
            pl.BlockSpec((None, 1, tn), lambda i, n: (i, 0, n)),
        ],
        out_specs=pl.BlockSpec((None, b, tn), lambda i, n: (i, 0, n)),
        out_shape=jax.ShapeDtypeStruct((depth, b, d3), F32),
        compiler_params=_params("parallel", "parallel"),
        name="adaln_modulation",
    )(c, mod_w, mod_b.reshape(depth, 1, d3))


def _layer_vec_specs(layer, d):
    ng = pl.BlockSpec((None, 1, d), lambda b, m: (layer, 0, 0))

    def mod(which):
        return pl.BlockSpec((None, None, None, 1, d), lambda b, m: (layer, b, which, 0, 0))
    return ng, mod


def _ret_in_kernel(x_ref, ng_ref, sc_ref, sh_ref, pos_ref, inv_ref, w_ref,
                   q_ref, k_ref, v_ref, g_ref, *, heads, hk, dv, tn):
    h = _norm_mod(x_ref, ng_ref, sc_ref, sh_ref)
    ang = pos_ref[...].astype(F32) * inv_ref[...]
    cos = jnp.cos(ang)
    sin = jnp.sin(ang)
    half = hk // 2
    dk = heads * hk
    for col0, o_ref, scale in ((0, q_ref, hk ** -0.5), (dk, k_ref, None)):
        for hd in range(heads):
            r = _dot(h, w_ref[:, col0 + hd * hk:col0 + (hd + 1) * hk])
            x1 = r[:, :half]
            x2 = r[:, half:]
            o1 = x1 * cos - x2 * sin
            o2 = x1 * sin + x2 * cos
            if scale is not None:
                o1 = o1 * scale
                o2 = o2 * scale
            o_ref[:, hd * hk:hd * hk + half] = o1.astype(BF16)
            o_ref[:, hd * hk + half:(hd + 1) * hk] = o2.astype(BF16)
    _project(h, w_ref, 2 * dk, dv, v_ref, tn)
    _project(h, w_ref, 2 * dk + dv, dv, g_ref, tn)


def _ret_in_proj(x, positions, norm_g, mod, layer, w_in, tm=512, tn=512):
    b, s, d = x.shape
    heads = RET_HEADS
    dk = d
    dv = 2 * d
    hk = dk // heads
    n_in = w_in.shape[1]
    half = hk // 2
    inv = (ROPE_BASE ** (-jnp.arange(half, dtype=F32) / half)).reshape(1, half)
    ng, modspec = _layer_vec_specs(layer, d)
    tok = lambda w: pl.BlockSpec((None, tm, w), lambda bi, m: (bi, m, 0))
    return pl.pallas_call(
        functools.partial(_ret_in_kernel, heads=heads, hk=hk, dv=dv, tn=tn),
        grid=(b, s // tm),
        in_specs=[tok(d), ng, modspec(1), modspec(0), tok(1), _resident((1, half)), _resident((d, n_in))],
        out_specs=[tok(dk), tok(dk), tok(dv), tok(dv)],
        out_shape=[jax.ShapeDtypeStruct((b, s, w), BF16) for w in (dk, dk, dv, dv)],
        compiler_params=_params("parallel", "parallel"),
        name="retention_in_proj",
    )(x, norm_g, mod, mod, positions.reshape(b, s, 1), inv, w_in)


def _ret_core_kernel(q_ref, k_ref, v_ref, g_ref, y_ref, state_ref, dmask_ref, *, heads, hk, hv, sup):
    log_gamma = [math.log1p(-(2.0 ** (-5.0 - h))) for h in range(heads)]

    @pl.when(pl.program_id(1) == 0)
    def _():
        state_ref[...] = jnp.zeros_like(state_ref)
        t = lax.broadcasted_iota(jnp.int32, (sup, sup), 0)
        s = lax.broadcasted_iota(jnp.int32, (sup, sup), 1)
        dist = jnp.abs(t - s).astype(F32)
        visible = (s // CHUNK) <= (t // CHUNK)
        for h in range(heads):
            dmask_ref[h] = jnp.where(visible, jnp.exp(log_gamma[h] * dist), 0.0)

    pos = lax.broadcasted_iota(jnp.int32, (sup, 1), 0).astype(F32)
    for h in range(heads):
        q = q_ref[:, h * hk:(h + 1) * hk]
        k = k_ref[:, h * hk:(h + 1) * hk]
        v = v_ref[:, h * hv:(h + 1) * hv]
        scores = lax.dot_general(q, k, NT_DIMS, preferred_element_type=F32) * dmask_ref[h]
        o = _dot(scores.astype(BF16), v)
        state = state_ref[h]
        q_decay = jnp.exp(log_gamma[h] * (pos + 1.0))
        o = o + q_decay * _dot(q, state.astype(BF16))
        k_decay = jnp.exp(log_gamma[h] * (sup - 1.0 - pos))
        k_in = (k.astype(F32) * k_decay).astype(BF16)
        state_ref[h] = math.exp(log_gamma[h] * sup) * state + lax.dot_general(
            k_in, v, TN_DIMS, preferred_element_type=F32)
        mu = jnp.mean(o, axis=-1, keepdims=True)
        oc = o - mu
        var = jnp.mean(oc * oc, axis=-1, keepdims=True)
        gate = _silu(g_ref[:, h * hv:(h + 1) * hv].astype(F32))
        y_ref[:, h * hv:(h + 1) * hv] = (oc * lax.rsqrt(var + EPS) * gate).astype(BF16)


def _ret_core(q, k, v, g):
    b, s, dk = q.shape
    dv = v.shape[-1]
    heads = RET_HEADS
    hk, hv = dk // heads, dv // heads
    sup = RET_SUPER
    tok = lambda w: pl.BlockSpec((None, sup, w), lambda bi, l: (bi, l, 0))
    return pl.pallas_call(
        functools.partial(_ret_core_kernel, heads=heads, hk=hk, hv=hv, sup=sup),
        grid=(b, s // sup),
        in_specs=[tok(dk), tok(dk), tok(dv), tok(dv)],
        out_specs=tok(dv),
        out_shape=jax.ShapeDtypeStruct((b, s, dv), BF16),
        scratch_shapes=[pltpu.VMEM((heads, hk, hv), F32), pltpu.VMEM((heads, sup, sup), F32)],
        compiler_params=_params("parallel", "arbitrary"),
        name="retention_core",
    )(q, k, v, g)


def _fox_in_kernel(x_ref, ng_ref, sc_ref, sh_ref, w_ref, wf_ref, bf_ref,
                   q_ref, k_ref, v_ref, g_ref, cum_ref, carry_ref, *, width, tn):
    @pl.when(pl.program_id(1) == 0)
    def _():
        carry_ref[...] = jnp.zeros_like(carry_ref)

    h = _norm_mod(x_ref, ng_ref, sc_ref, sh_ref)
    for idx, o_ref in enumerate((q_ref, k_ref, v_ref, g_ref)):
        _project(h, w_ref, idx * width, width, o_ref, tn)
    log_f = _log_sigmoid(_dot(h, wf_ref[...]) + bf_ref[...])
    rows = log_f.shape[0]
    tri = jnp.where(_lower_tri(rows), 1.0, 0.0).astype(BF16)
    cum = _tri_cumsum(tri, log_f) + carry_ref[...]
    carry_ref[...] = cum[rows - 1:rows, :]
    cum_ref[...] = cum


def _fox_in_proj(x, norm_g, mod, layer, w_in, b_f, tm=512, tn=512):
    b, s, d = x.shape
    heads = FOX_HEADS
    width = (w_in.shape[1] - heads) // 4
    w_main = w_in[:, :4 * width]
    w_f = jnp.pad(w_in[:, 4 * width:], ((0, 0), (0, LANES - heads)))
    bias = jnp.pad(b_f.astype(F32), (0, LANES - heads)).reshape(1, LANES)
    ng, modspec = _layer_vec_specs(layer, d)
    tok = lambda w: pl.BlockSpec((None, tm, w), lambda bi, m: (bi, m, 0))
    return pl.pallas_call(
        functools.partial(_fox_in_kernel, width=width, tn=tn),
        grid=(b, s // tm),
        in_specs=[tok(d), ng, modspec(1), modspec(0),
                  _resident((d, 4 * width)), _resident((d, LANES)), _resident((1, LANES))],
        out_specs=[tok(width)] * 4 + [tok(LANES)],
        out_shape=[jax.ShapeDtypeStruct((b, s, width), BF16)] * 4 + [jax.ShapeDtypeStruct((b, s, LANES), F32)],
        scratch_shapes=[pltpu.VMEM((1, LANES), F32)],
        compiler_params=_params("parallel", "arbitrary"),
        name="fox_in_proj",
    )(x, norm_g, mod, mod, w_main, w_f, bias)


def _fox_flash_kernel(q_ref, k_ref, v_ref, g_ref, cum_ref, qg_ref, kg_ref, y_ref,
                      kx_ref, vx_ref, qx_ref, m_ref, acc_ref, *, bq, hd, seq):
    pair = pl.program_id(1)
    qi = pl.program_id(2)
    lane = lax.broadcasted_iota(jnp.int32, (1, LANES), 1)
    first = lane < hd
    sels = (first, jnp.logical_not(first))
    aux0 = (hd, 0)

    def head_norm(x, gain):
        sq = x * x
        s0 = jnp.sum(jnp.where(first, sq, 0.0), axis=-1, keepdims=True)
        s1 = jnp.sum(jnp.where(first, 0.0, sq), axis=-1, keepdims=True)
        ms = jnp.where(first, s0, s1) * (1.0 / hd)
        return x * lax.rsqrt(ms + EPS) * gain

    def head_cum(cum, j):
        return jnp.sum(jnp.where(lane == 2 * pair + j, cum, 0.0), axis=-1, keepdims=True) * LOG2E

    def with_bias(data, j, col, col_first):
        pieces = [p.astype(F32) for p in _split3(col)]
        vals = pieces + [1.0, 1.0, 1.0] if col_first else [1.0, 1.0, 1.0] + pieces
        out = jnp.where(sels[j], data, 0.0)
        for t, val in enumerate(vals):
            out = jnp.where(lane == aux0[j] + t, val, out)
        return out.astype(BF16)

    @pl.when(qi == 0)
    def _():
        one = jnp.ones((), BF16)
        for r0 in range(0, seq, bq):
            kn = head_norm(k_ref[r0:r0 + bq, :].astype(F32), kg_ref[...])
            cum = cum_ref[r0:r0 + bq, :]
            v = v_ref[r0:r0 + bq, :]
            for j in range(2):
                kx_ref[j, r0:r0 + bq, :] = with_bias(kn, j, -head_cum(cum, j), True)
                vx_ref[j, r0:r0 + bq, :] = jnp.where(sels[j], v, one)

    q0 = pl.multiple_of(qi * bq, bq)
    qn = head_norm(q_ref[...].astype(F32), qg_ref[...]) * (hd ** -0.5 * LOG2E)
    cum_q = cum_ref[pl.ds(q0, bq), :]
    for j in range(2):
        qx_ref[j] = with_bias(qn, j, head_cum(cum_q, j), False)
    m_ref[...] = jnp.full_like(m_ref, NEG_BIG)
    acc_ref[...] = jnp.zeros_like(acc_ref)
    causal = _lower_tri(bq)

    def step(kb, masked):
        r0 = pl.multiple_of(kb * bq, bq)
        logits = [lax.dot_general(qx_ref[j], kx_ref[j, pl.ds(r0, bq), :], NT_DIMS, preferred_element_type=F32)
                  for j in range(2)]
        for j in range(2):
            s = jnp.where(causal, logits[j], NEG_BIG) if masked else logits[j]
            m_prev = m_ref[j]
            m_new = jnp.maximum(m_prev, jnp.max(s, axis=-1, keepdims=True))
            p = jnp.exp2(s - pltpu.repeat(m_new, bq // LANES, axis=1))
            acc_ref[j] = jnp.exp2(m_prev - m_new) * acc_ref[j] + _dot(p.astype(BF16), vx_ref[j, pl.ds(r0, bq), :])
            m_ref[j] = m_new

    def body(kb, carry):
        step(kb, False)
        return carry

    lax.fori_loop(0, qi, body, 0)
    step(qi, True)
    num = jnp.where(first, acc_ref[0], acc_ref[1])
    den = jnp.where(first, pltpu.roll(acc_ref[0], hd, axis=1), pltpu.roll(acc_ref[1], hd, axis=1))
    y_ref[...] = (num / den * _silu(g_ref[...].astype(F32))).astype(BF16)


def _fox_flash(q, k, v, g, cum, q_gain, k_gain, bq=512):
    b, s, width = q.shape
    heads = FOX_HEADS
    hd = width // heads
    assert 2 * hd == LANES
    pairs = heads // 2
    qg = jnp.tile(q_gain.astype(F32), 2).reshape(1, LANES)
    kg = jnp.tile(k_gain.astype(F32), 2).reshape(1, LANES)
    blk = pl.BlockSpec((None, bq, LANES), lambda bi, p, i: (bi, i, p))
    full = pl.BlockSpec((None, s, LANES), lambda bi, p, i: (bi, 0, p))
    gain = pl.BlockSpec((1, LANES), lambda bi, p, i: (0, 0))
    return pl.pallas_call(
        functools.partial(_fox_flash_kernel, bq=bq, hd=hd, seq=s),
        grid=(b, pairs, s // bq),
        in_specs=[blk, full, full, blk, pl.BlockSpec((None, s, LANES), lambda bi, p, i: (bi, 0, 0)), gain, gain],
        out_specs=blk,
        out_shape=jax.ShapeDtypeStruct((b, s, width), BF16),
        scratch_shapes=[pltpu.VMEM((2, s, LANES), BF16), pltpu.VMEM((2, s, LANES), BF16),
                        pltpu.VMEM((2, bq, LANES), BF16), pltpu.VMEM((2, bq, LANES), F32),
                        pltpu.VMEM((2, bq, LANES), F32)],
        compiler_params=_params("parallel", "parallel", "arbitrary"),
        name="fox_flash_attention",
    )(q, k, v, g, cum, qg, kg)


def _gla_in_kernel(x_ref, ng_ref, sc_ref, sh_ref, w_ref, wr_ref, wg2_ref, bg_ref,
                   q_ref, k_ref, v_ref, g_ref, la_ref, *, dk, dv, tn):
    h = _norm_mod(x_ref, ng_ref, sc_ref, sh_ref)
    _project(h, w_ref, 0, dk, q_ref, tn)
    _project(h, w_ref, dk, dk, k_ref, tn)
    _project(h, w_ref, 2 * dk, dv, v_ref, tn)
    _project(h, w_ref, 2 * dk + dv, dv, g_ref, tn)
    r = _dot(h, wr_ref[...])
    z = _dot(r.astype(BF16), wg2_ref[...]) + bg_ref[...]
    la_ref[...] = _log_sigmoid(z) * (1.0 / GLA_TAU)


def _gla_in_proj(x, norm_g, mod, layer, w_in, w_gate2, b_gate, tm=512, tn=512):
    b, s, d = x.shape
    dk, dv, rank = d // 2, d, GLA_RANK
    main = 2 * dk + 2 * dv
    w_main = w_in[:, :main]
    w_r = jnp.pad(w_in[:, main:], ((0, 0), (0, LANES - rank)))
    w_g2 = jnp.pad(w_gate2, ((0, LANES - rank), (0, 0)))
    ng, modspec = _layer_vec_specs(layer, d)
    tok = lambda w, : pl.BlockSpec((None, tm, w), lambda bi, m: (bi, m, 0))
    return pl.pallas_call(
        functools.partial(_gla_in_kernel, dk=dk, dv=dv, tn=tn),
        grid=(b, s // tm),
        in_specs=[tok(d), ng, modspec(1), modspec(0), _resident((d, main)), _resident((d, LANES)),
                  _resident((LANES, dk)), _resident((1, dk))],
        out_specs=[tok(dk), tok(dk), tok(dv), tok(dv), tok(dk)],
        out_shape=[jax.ShapeDtypeStruct((b, s, w), BF16) for w in (dk, dk, dv, dv)]
                  + [jax.ShapeDtypeStruct((b, s, dk), F32)],
        compiler_params=_params("parallel", "parallel"),
        name="gla_in_proj",
    )(x, norm_g, mod, mod, w_main, w_r, w_g2, b_gate.astype(F32).reshape(1, dk))


def _gla_core_kernel(q_ref, k_ref, v_ref, g_ref, la_ref, y_ref, state_ref, *, heads, hk, hv, nchunk):
    @pl.when(pl.program_id(1) == 0)
    def _():
        state_ref[...] = jnp.zeros_like(state_ref)

    causal = _lower_tri(CHUNK)
    tri = jnp.where(causal, 1.0, 0.0).astype(BF16)
    scale = hk ** -0.5

    def chunk(c, carry):
        r0 = pl.multiple_of(c * CHUNK, CHUNK)
        rows = pl.ds(r0, CHUNK)
        cb = _tri_cumsum(tri, la_ref[rows, :])
        cb_last = cb[CHUNK - 1:CHUNK, :]
        eb = jnp.exp(cb)
        enb = jnp.exp(-cb)
        qf = q_ref[rows, :].astype(F32) * scale
        kf = k_ref[rows, :].astype(F32)
        q_e = (qf * eb).astype(BF16)
        q_n = (qf * enb).astype(BF16)
        k_n = (kf * enb).astype(BF16)
        k_e = (kf * eb).astype(BF16)
        k_in = (kf * jnp.exp(cb_last - cb)).astype(BF16)
        state_decay = jnp.exp(cb_last)
        for h in range(heads):
            ks = slice(h * hk, (h + 1) * hk)
            vs = slice(h * hv, (h + 1) * hv)
            a_causal = lax.dot_general(q_e[:, ks], k_n[:, ks], NT_DIMS, preferred_element_type=F32)
            a_anti = lax.dot_general(q_n[:, ks], k_e[:, ks], NT_DIMS, preferred_element_type=F32)
            attn = jnp.where(causal, a_causal, a_anti).astype(BF16)
            v = v_ref[rows, vs]
            state_t = state_ref[h]
            o = _dot(attn, v) + lax.dot_general(q_e[:, ks], state_t.astype(BF16), NT_DIMS,
                                                preferred_element_type=F32)
            state_ref[h] = state_t * state_decay[:, ks] + lax.dot_general(
                v, k_in[:, ks], TN_DIMS, preferred_element_type=F32)
            ms = jnp.mean(o * o, axis=-1, keepdims=True)
            gate = _silu(g_ref[rows, vs].astype(F32))
            y_ref[rows, vs] = (o * lax.rsqrt(ms + EPS) * gate).astype(BF16)
        return carry

    lax.fori_loop(0, nchunk, chunk, 0)


def _gla_core(q, k, v, g, la, tl=512):
    b, s, dk = q.shape
    dv = v.shape[-1]
    heads = GLA_HEADS
    hk, hv = dk // heads, dv // heads
    tok = lambda w: pl.BlockSpec((None, tl, w), lambda bi, l: (bi, l, 0))
    return pl.pallas_call(
        functools.partial(_gla_core_kernel, heads=heads, hk=hk, hv=hv, nchunk=tl // CHUNK),
        grid=(b, s // tl),
        in_specs=[tok(dk), tok(dk), tok(dv), tok(dv), tok(dk)],
        out_specs=tok(dv),
        out_shape=jax.ShapeDtypeStruct((b, s, dv), BF16),
        scratch_shapes=[pltpu.VMEM((heads, hv, hk), F32)],
        compiler_params=_params("parallel", "arbitrary"),
        name="gla_core",
    )(q, k, v, g, la)


def _out_kernel(y_ref, w_ref, x_ref, gate_ref, fg_ref, o_ref, *, final):
    xn = x_ref[...] + gate_ref[...] * _dot(y_ref[...], w_ref[...])
    if final:
        ms = jnp.mean(xn * xn, axis=-1, keepdims=True)
        xn = xn * lax.rsqrt(ms + EPS) * fg_ref[...]
    o_ref[...] = xn


def _out_proj(y, w_out, x, mod, layer, final_g, final, tm=512):
    b, s, d = x.shape
    kdim = y.shape[-1]
    _, modspec = _layer_vec_specs(layer, d)
    tok = lambda w: pl.BlockSpec((None, tm, w), lambda bi, m: (bi, m, 0))
    return pl.pallas_call(
        functools.partial(_out_kernel, final=final),
        grid=(b, s // tm),
        in_specs=[tok(kdim), _resident((kdim, d)), tok(d), modspec(2), _resident((1, d))],
        out_specs=tok(d),
        out_shape=jax.ShapeDtypeStruct((b, s, d), F32),
        compiler_params=_params("parallel", "parallel"),
        name="out_proj_residual",
    )(y, w_out, x, mod, final_g)


def kernel(x, c, positions, mod_w, mod_b, norm_g, ret_w_in, ret_w_out, fox_w_in, fox_b_f, fox_q_gain,
           fox_k_gain, fox_w_out, gla_w_in, gla_w_gate2, gla_b_gate, gla_w_out, final_g):
    depth, d, _ = mod_w.shape
    b = x.shape[0]
    mod = _modulation(c, mod_w, mod_b).reshape(depth, b, 3, 1, d)
    ng = norm_g.astype(F32).reshape(depth, 1, d)
    fg = final_g.astype(F32).reshape(1, d)
    for i in range(depth):
        j = i // N_MIXERS
        kind = i % N_MIXERS
        if kind == 0:
            q, k, v, g = _ret_in_proj(x, positions, ng, mod, i, ret_w_in[j].astype(BF16))
            y = _ret_core(q, k, v, g)
            w_out = ret_w_out[j]
        elif kind == 1:
            q, k, v, g, cum = _fox_in_proj(x, ng, mod, i, fox_w_in[j].astype(BF16), fox_b_f[j])
            y = _fox_flash(q, k, v, g, cum, fox_q_gain[j], fox_k_gain[j])
            w_out = fox_w_out[j]
        else:
            q, k, v, g, la = _gla_in_proj(x, ng, mod, i, gla_w_in[j].astype(BF16),
                                          gla_w_gate2[j].astype(BF16), gla_b_gate[j])
            y = _gla_core(q, k, v, g, la)
            w_out = gla_w_out[j]
        x = _out_proj(y, w_out.astype(BF16), x, mod, i, fg, final=(i == depth - 1))
    return x
```

```python
import functools
import math

import jax
import jax.numpy as jnp
from jax import lax
from jax.experimental import pallas as pl
from jax.experimental.pallas import tpu as pltpu

F32 = jnp.float32
BF16 = jnp.bfloat16

EPS = 1e-6
N_MIXERS = 3
CHUNK = 64
RET_HEADS = 4
ROPE_BASE = 10000.0
FOX_HEADS = 16
GLA_HEADS = 4
GLA_RANK = 16
GLA_TAU = 16.0

LANES = 128
RET_SUPER = 256
NEG_BIG = -1e30
LOG2E = math.log2(math.e)
VMEM_LIMIT = 56 * 1024 * 1024

NT_DIMS = (((1,), (1,)), ((), ()))
TN_DIMS = (((0,), (0,)), ((), ()))


def _silu(x):
    return x * (1.0 / (1.0 + jnp.exp(-x)))


def _log_sigmoid(z):
    return jnp.minimum(z, 0.0) - jnp.log1p(jnp.exp(-jnp.abs(z)))


def _dot(a, b):
    return jnp.dot(a, b, preferred_element_type=F32)


def _split3(x):
    hi = x.astype(BF16)
    r1 = x - hi.astype(F32)
    mid = r1.astype(BF16)
    lo = (r1 - mid.astype(F32)).astype(BF16)
    return hi, mid, lo


def _tri_cumsum(tri, x, pieces=3):
    parts = _split3(x)[:pieces]
    out = _dot(tri, parts[0])
    for part in parts[1:]:
        out = out + _dot(tri, part)
    return out


def _lower_tri(n):
    row = lax.broadcasted_iota(jnp.int32, (n, n), 0)
    col = lax.broadcasted_iota(jnp.int32, (n, n), 1)
    return row >= col


def _norm_mod(x_ref, ng_ref, sc_ref, sh_ref):
    x = x_ref[...]
    ms = jnp.mean(x * x, axis=-1, keepdims=True)
    y = x * lax.rsqrt(ms + EPS) * ng_ref[...]
    return (y * (1.0 + sc_ref[...]) + sh_ref[...]).astype(BF16)


def _project(h, w_ref, col0, width, o_ref, tn):
    for n0 in range(0, width, tn):
        o_ref[:, n0:n0 + tn] = _dot(h, w_ref[:, col0 + n0:col0 + n0 + tn]).astype(o_ref.dtype)


def _params(*sem):
    return pltpu.CompilerParams(dimension_semantics=sem, vmem_limit_bytes=VMEM_LIMIT)


def _resident(shape):
    nd = len(shape)
    return pl.BlockSpec(shape, lambda *_: (0,) * nd, pipeline_mode=pl.Buffered(1))


def _mod_kernel(c_ref, w_ref, b_ref, o_ref):
    a = _silu(c_ref[...])
    o_ref[...] = jnp.dot(a, w_ref[...], preferred_element_type=F32,
                         precision=lax.Precision.HIGHEST) + b_ref[...]


def _modulation(c, mod_w, mod_b, tn=1024):
    depth, d, d3 = mod_w.shape
    b = c.shape[0]
    return pl.pallas_call(
        _mod_kernel,
        grid=(depth, d3 // tn),
        in_specs=[
            pl.BlockSpec((b, d), lambda i, n: (0, 0)),
            pl.BlockSpec((None, d, tn), lambda i, n: (i, 0, n)),
            pl.BlockSpec((None, 1, tn), lambda i, n: (i, 0, n)),
        ],
        out_specs=pl.BlockSpec((None, b, tn), lambda i, n: (i, 0, n)),
        out_shape=jax.ShapeDtypeStruct((depth, b, d3), F32),
        compiler_params=_params("parallel", "parallel"),
        name="adaln_modulation",
    )(c, mod_w, mod_b.reshape(depth, 1, d3))


def _layer_vec_specs(layer, d):
    ng = pl.BlockSpec((None, 1, d), lambda b, m: (layer, 0, 0))

    def mod(which):
        return pl.BlockSpec((None, None, None, 1, d), lambda b, m: (layer, b, which, 0, 0))
    return ng, mod


def _rope_kernel(pos_ref, inv_ref, cos_ref, sin_ref):
    ang = pos_ref[...].astype(F32) * inv_ref[...]
    cos_ref[...] = jnp.cos(ang)
    sin_ref[...] = jnp.sin(ang)


def _rope_tables(positions, half, tm=1024):
    b, s = positions.shape
    inv = (ROPE_BASE ** (-jnp.arange(half, dtype=F32) / half)).reshape(1, half)
    tok = lambda w: pl.BlockSpec((None, tm, w), lambda bi, m: (bi, m, 0))
    return pl.pallas_call(
        _rope_kernel,
        grid=(b, s // tm),
        in_specs=[tok(1), pl.BlockSpec((1, half), lambda bi, m: (0, 0))],
        out_specs=[tok(half), tok(half)],
        out_shape=[jax.ShapeDtypeStruct((b, s, half), F32)] * 2,
        compiler_params=_params("parallel", "parallel"),
        name="rope_tables",
    )(positions.reshape(b, s, 1), inv)


def _ret_in_kernel(x_ref, ng_ref, sc_ref, sh_ref, cos_ref, sin_ref, w_ref,
                   q_ref, k_ref, v_ref, g_ref, *, heads, hk, dv, tn):
    h = _norm_mod(x_ref, ng_ref, sc_ref, sh_ref)
    cos = cos_ref[...]
    sin = sin_ref[...]
    half = hk // 2
    dk = heads * hk
    _project(h, w_ref, 2 * dk, dv, v_ref, tn)
    for col0, o_ref, scale in ((0, q_ref, hk ** -0.5), (dk, k_ref, None)):
        for hd in range(heads):
            r = _dot(h, w_ref[:, col0 + hd * hk:col0 + (hd + 1) * hk])
            x1 = r[:, :half]
            x2 = r[:, half:]
            o1 = x1 * cos - x2 * sin
            o2 = x1 * sin + x2 * cos
            if scale is not None:
                o1 = o1 * scale
                o2 = o2 * scale
            o_ref[:, hd * hk:hd * hk + half] = o1.astype(BF16)
            o_ref[:, hd * hk + half:(hd + 1) * hk] = o2.astype(BF16)
    _project(h, w_ref, 2 * dk + dv, dv, g_ref, tn)


def _ret_in_proj(x, cos, sin, norm_g, mod, layer, w_in, tm=512, tn=512):
    b, s, d = x.shape
    heads = RET_HEADS
    dk = d
    dv = 2 * d
    hk = dk // heads
    n_in = w_in.shape[1]
    half = hk // 2
    ng, modspec = _layer_vec_specs(layer, d)
    tok = lambda w: pl.BlockSpec((None, tm, w), lambda bi, m: (bi, m, 0))
    return pl.pallas_call(
        functools.partial(_ret_in_kernel, heads=heads, hk=hk, dv=dv, tn=tn),
        grid=(b, s // tm),
        in_specs=[tok(d), ng, modspec(1), modspec(0), tok(half), tok(half), _resident((d, n_in))],
        out_specs=[tok(dk), tok(dk), tok(dv), tok(dv)],
        out_shape=[jax.ShapeDtypeStruct((b, s, w), BF16) for w in (dk, dk, dv, dv)],
        compiler_params=_params("parallel", "parallel"),
        name="retention_in_proj",
    )(x, norm_g, mod, mod, cos, sin, w_in)


def _residual_out(y_ref, w_ref, x_ref, gate_ref, fg_ref, o_ref, final):
    kdim = y_ref.shape[-1]
    half = kdim // 2
    r = _dot(y_ref[:, :half], w_ref[:half, :]) + _dot(y_ref[:, half:], w_ref[half:, :])
    xn = x_ref[...] + gate_ref[...] * r
    if final:
        ms = jnp.mean(xn * xn, axis=-1, keepdims=True)
        xn = xn * lax.rsqrt(ms + EPS) * fg_ref[...]
    o_ref[...] = xn


def _ret_core_kernel(q_ref, k_ref, v_ref, g_ref, x_ref, w_ref, gate_ref, fg_ref, o_ref,
                     state_ref, dmask_ref, qdec_ref, kdec_ref, y_ref, *, heads, hk, hv, sup, final):
    log_gamma = [math.log1p(-(2.0 ** (-5.0 - h))) for h in range(heads)]

    @pl.when(pl.program_id(1) == 0)
    def _():
        state_ref[...] = jnp.zeros_like(state_ref)
        t = lax.broadcasted_iota(jnp.int32, (sup, sup), 0)
        s = lax.broadcasted_iota(jnp.int32, (sup, sup), 1)
        dist = jnp.abs(t - s).astype(F32)
        visible = (s // CHUNK) <= (t // CHUNK)
        pos = lax.broadcasted_iota(jnp.int32, (sup, hk), 0).astype(F32)
        for h in range(heads):
            dmask_ref[h] = jnp.where(visible, jnp.exp(log_gamma[h] * dist), 0.0)
            qdec_ref[h] = jnp.exp(log_gamma[h] * (pos + 1.0))
            kdec_ref[h] = jnp.exp(log_gamma[h] * (sup - 1.0 - pos))

    for h in range(heads):
        q = q_ref[:, h * hk:(h + 1) * hk]
        k = k_ref[:, h * hk:(h + 1) * hk]
        v = v_ref[:, h * hv:(h + 1) * hv]
        scores = lax.dot_general(q, k, NT_DIMS, preferred_element_type=F32) * dmask_ref[h]
        state = state_ref[h]
        q_in = (q.astype(F32) * qdec_ref[h]).astype(BF16)
        o = _dot(scores.astype(BF16), v) + _dot(q_in, state.astype(BF16))
        k_in = (k.astype(F32) * kdec_ref[h]).astype(BF16)
        state_ref[h] = math.exp(log_gamma[h] * sup) * state + lax.dot_general(
            k_in, v, TN_DIMS, preferred_element_type=F32)
        mu = jnp.mean(o, axis=-1, keepdims=True)
        oc = o - mu
        var = jnp.mean(oc * oc, axis=-1, keepdims=True)
        gate = _silu(g_ref[:, h * hv:(h + 1) * hv].astype(F32))
        y_ref[:, h * hv:(h + 1) * hv] = (oc * lax.rsqrt(var + EPS) * gate).astype(BF16)
    _residual_out(y_ref, w_ref, x_ref, gate_ref, fg_ref, o_ref, final)


def _ret_core(q, k, v, g, x, w_out, mod, layer, final_g, final):
    b, s, dk = q.shape
    dv = v.shape[-1]
    d = x.shape[-1]
    heads = RET_HEADS
    hk, hv = dk // heads, dv // heads
    sup = RET_SUPER
    _, modspec = _layer_vec_specs(layer, d)
    tok = lambda w: pl.BlockSpec((None, sup, w), lambda bi, l: (bi, l, 0))
    return pl.pallas_call(
        functools.partial(_ret_core_kernel, heads=heads, hk=hk, hv=hv, sup=sup, final=final),
        grid=(b, s // sup),
        in_specs=[tok(dk), tok(dk), tok(dv), tok(dv), tok(d), _resident((dv, d)), modspec(2), _resident((1, d))],
        out_specs=tok(d),
        out_shape=jax.ShapeDtypeStruct((b, s, d), F32),
        scratch_shapes=[pltpu.VMEM((heads, hk, hv), F32), pltpu.VMEM((heads, sup, sup), F32),
                        pltpu.VMEM((heads, sup, hk), F32), pltpu.VMEM((heads, sup, hk), F32),
                        pltpu.VMEM((sup, dv), BF16)],
        compiler_params=_params("parallel", "arbitrary"),
        name="retention_core_out",
    )(q, k, v, g, x, w_out, mod, final_g)


def _fox_in_kernel(x_ref, ng_ref, sc_ref, sh_ref, w_ref, wf_ref, bf_ref,
                   q_ref, k_ref, v_ref, g_ref, cum_ref, carry_ref, *, width, tn):
    @pl.when(pl.program_id(1) == 0)
    def _():
        carry_ref[...] = jnp.zeros_like(carry_ref)

    h = _norm_mod(x_ref, ng_ref, sc_ref, sh_ref)
    for idx, o_ref in enumerate((q_ref, k_ref, v_ref, g_ref)):
        _project(h, w_ref, idx * width, width, o_ref, tn)
    log_f = _log_sigmoid(_dot(h, wf_ref[...]) + bf_ref[...])
    rows = log_f.shape[0]
    tri = jnp.where(_lower_tri(rows), 1.0, 0.0).astype(BF16)
    cum = _tri_cumsum(tri, log_f) + carry_ref[...]
    carry_ref[...] = cum[rows - 1:rows, :]
    cum_ref[...] = cum


def _fox_in_proj(x, norm_g, mod, layer, w_in, b_f, tm=512, tn=512):
    b, s, d = x.shape
    heads = FOX_HEADS
    width = (w_in.shape[1] - heads) // 4
    w_main = w_in[:, :4 * width]
    w_f = jnp.pad(w_in[:, 4 * width:], ((0, 0), (0, LANES - heads)))
    bias = jnp.pad(b_f.astype(F32), (0, LANES - heads)).reshape(1, LANES)
    ng, modspec = _layer_vec_specs(layer, d)
    tok = lambda w: pl.BlockSpec((None, tm, w), lambda bi, m: (bi, m, 0))
    return pl.pallas_call(
        functools.partial(_fox_in_kernel, width=width, tn=tn),
        grid=(b, s // tm),
        in_specs=[tok(d), ng, modspec(1), modspec(0),
                  _resident((d, 4 * width)), _resident((d, LANES)), _resident((1, LANES))],
        out_specs=[tok(width)] * 4 + [tok(LANES)],
        out_shape=[jax.ShapeDtypeStruct((b, s, width), BF16)] * 4 + [jax.ShapeDtypeStruct((b, s, LANES), F32)],
        scratch_shapes=[pltpu.VMEM((1, LANES), F32)],
        compiler_params=_params("parallel", "arbitrary"),
        name="fox_in_proj",
    )(x, norm_g, mod, mod, w_main, w_f, bias)


def _fox_flash_kernel(q_ref, k_ref, v_ref, g_ref, cum_ref, qg_ref, kg_ref, y_ref,
                      kx_ref, vx_ref, qx_ref, m_ref, acc_ref, *, bq, hd, seq):
    pair = pl.program_id(1)
    qi = pl.program_id(2)
    lane = lax.broadcasted_iota(jnp.int32, (1, LANES), 1)
    first = lane < hd
    sels = (first, jnp.logical_not(first))
    aux0 = (hd, 0)

    def head_norm(x, gain):
        sq = x * x
        s0 = jnp.sum(jnp.where(first, sq, 0.0), axis=-1, keepdims=True)
        s1 = jnp.sum(jnp.where(first, 0.0, sq), axis=-1, keepdims=True)
        ms = jnp.where(first, s0, s1) * (1.0 / hd)
        return x * lax.rsqrt(ms + EPS) * gain

    def head_cum(cum, j):
        return jnp.sum(jnp.where(lane == 2 * pair + j, cum, 0.0), axis=-1, keepdims=True) * LOG2E

    def with_bias(data, j, col, col_first):
        pieces = [p.astype(F32) for p in _split3(col)]
        vals = pieces + [1.0, 1.0, 1.0] if col_first else [1.0, 1.0, 1.0] + pieces
        out = jnp.where(sels[j], data, 0.0)
        for t, val in enumerate(vals):
            out = jnp.where(lane == aux0[j] + t, val, out)
        return out.astype(BF16)

    @pl.when(qi == 0)
    def _():
        one = jnp.ones((), BF16)
        for r0 in range(0, seq, bq):
            kn = head_norm(k_ref[r0:r0 + bq, :].astype(F32), kg_ref[...])
            cum = cum_ref[r0:r0 + bq, :]
            v = v_ref[r0:r0 + bq, :]
            for j in range(2):
                kx_ref[j, r0:r0 + bq, :] = with_bias(kn, j, -head_cum(cum, j), True)
                vx_ref[j, r0:r0 + bq, :] = jnp.where(sels[j], v, one)

    q0 = pl.multiple_of(qi * bq, bq)
    qn = head_norm(q_ref[...].astype(F32), qg_ref[...]) * (hd ** -0.5 * LOG2E)
    cum_q = cum_ref[pl.ds(q0, bq), :]
    for j in range(2):
        qx_ref[j] = with_bias(qn, j, head_cum(cum_q, j), False)
    m_ref[...] = jnp.full_like(m_ref, NEG_BIG)
    acc_ref[...] = jnp.zeros_like(acc_ref)
    causal = _lower_tri(bq)

    def step(kb, masked):
        r0 = pl.multiple_of(kb * bq, bq)
        logits = [lax.dot_general(qx_ref[j], kx_ref[j, pl.ds(r0, bq), :], NT_DIMS, preferred_element_type=F32)
                  for j in range(2)]
        for j in range(2):
            s = jnp.where(causal, logits[j], NEG_BIG) if masked else logits[j]
            m_prev = m_ref[j]
            m_new = jnp.maximum(m_prev, jnp.max(s, axis=-1, keepdims=True))
            p = jnp.exp2(s - jnp.concatenate([m_new] * (bq // LANES), axis=1))
            acc_ref[j] = jnp.exp2(m_prev - m_new) * acc_ref[j] + _dot(p.astype(BF16), vx_ref[j, pl.ds(r0, bq), :])
            m_ref[j] = m_new

    def body(kb, carry):
        step(kb, False)
        return carry

    lax.fori_loop(0, qi, body, 0)
    step(qi, True)
    num = jnp.where(first, acc_ref[0], acc_ref[1])
    den = jnp.where(first, pltpu.roll(acc_ref[0], hd, axis=1), pltpu.roll(acc_ref[1], hd, axis=1))
    y_ref[...] = (num / den * _silu(g_ref[...].astype(F32))).astype(BF16)


def _fox_flash(q, k, v, g, cum, q_gain, k_gain, bq=512):
    b, s, width = q.shape
    heads = FOX_HEADS
    hd = width // heads
    assert 2 * hd == LANES
    pairs = heads // 2
    qg = jnp.tile(q_gain.astype(F32), 2).reshape(1, LANES)
    kg = jnp.tile(k_gain.astype(F32), 2).reshape(1, LANES)
    blk = pl.BlockSpec((None, bq, LANES), lambda bi, p, i: (bi, i, p))
    full = pl.BlockSpec((None, s, LANES), lambda bi, p, i: (bi, 0, p))
    gain = pl.BlockSpec((1, LANES), lambda bi, p, i: (0, 0))
    return pl.pallas_call(
        functools.partial(_fox_flash_kernel, bq=bq, hd=hd, seq=s),
        grid=(b, pairs, s // bq),
        in_specs=[blk, full, full, blk, pl.BlockSpec((None, s, LANES), lambda bi, p, i: (bi, 0, 0)), gain, gain],
        out_specs=blk,
        out_shape=jax.ShapeDtypeStruct((b, s, width), BF16),
        scratch_shapes=[pltpu.VMEM((2, s, LANES), BF16), pltpu.VMEM((2, s, LANES), BF16),
                        pltpu.VMEM((2, bq, LANES), BF16), pltpu.VMEM((2, bq, LANES), F32),
                        pltpu.VMEM((2, bq, LANES), F32)],
        compiler_params=_params("parallel", "parallel", "arbitrary"),
        name="fox_flash_attention",
    )(q, k, v, g, cum, qg, kg)


def _gla_in_kernel(x_ref, ng_ref, sc_ref, sh_ref, w_ref, wr_ref, wg2_ref, bg_ref,
                   q_ref, k_ref, v_ref, g_ref, la_ref, *, dk, dv, tn):
    h = _norm_mod(x_ref, ng_ref, sc_ref, sh_ref)
    _project(h, w_ref, 0, dk, q_ref, tn)
    _project(h, w_ref, dk, dk, k_ref, tn)
    _project(h, w_ref, 2 * dk, dv, v_ref, tn)
    _project(h, w_ref, 2 * dk + dv, dv, g_ref, tn)
    r = _dot(h, wr_ref[...])
    z = _dot(r.astype(BF16), wg2_ref[...]) + bg_ref[...]
    la_ref[...] = _log_sigmoid(z) * (1.0 / GLA_TAU)


def _gla_in_proj(x, norm_g, mod, layer, w_in, w_gate2, b_gate, tm=512, tn=512):
    b, s, d = x.shape
    dk, dv, rank = d // 2, d, GLA_RANK
    main = 2 * dk + 2 * dv
    w_main = w_in[:, :main]
    w_r = jnp.pad(w_in[:, main:], ((0, 0), (0, LANES - rank)))
    w_g2 = jnp.pad(w_gate2, ((0, LANES - rank), (0, 0)))
    ng, modspec = _layer_vec_specs(layer, d)
    tok = lambda w, : pl.BlockSpec((None, tm, w), lambda bi, m: (bi, m, 0))
    return pl.pallas_call(
        functools.partial(_gla_in_kernel, dk=dk, dv=dv, tn=tn),
        grid=(b, s // tm),
        in_specs=[tok(d), ng, modspec(1), modspec(0), _resident((d, main)), _resident((d, LANES)),
                  _resident((LANES, dk)), _resident((1, dk))],
        out_specs=[tok(dk), tok(dk), tok(dv), tok(dv), tok(dk)],
        out_shape=[jax.ShapeDtypeStruct((b, s, w), BF16) for w in (dk, dk, dv, dv)]
                  + [jax.ShapeDtypeStruct((b, s, dk), F32)],
        compiler_params=_params("parallel", "parallel"),
        name="gla_in_proj",
    )(x, norm_g, mod, mod, w_main, w_r, w_g2, b_gate.astype(F32).reshape(1, dk))


def _gla_core_kernel(q_ref, k_ref, v_ref, g_ref, la_ref, x_ref, w_ref, gate_ref, fg_ref, o_ref,
                     state_ref, y_ref, *, heads, hk, hv, nchunk, final):
    @pl.when(pl.program_id(1) == 0)
    def _():
        state_ref[...] = jnp.zeros_like(state_ref)

    tile = nchunk * CHUNK
    row = lax.broadcasted_iota(jnp.int32, (tile, tile), 0)
    col = lax.broadcasted_iota(jnp.int32, (tile, tile), 1)
    same = (row // CHUNK) == (col // CHUNK)
    causal = row >= col
    tri = jnp.where(jnp.logical_and(same, causal), 1.0, 0.0).astype(BF16)
    scale = hk ** -0.5
    dk = heads * hk
    chunk_rows = [slice(c * CHUNK, (c + 1) * CHUNK) for c in range(nchunk)]

    cb = _tri_cumsum(tri, la_ref[...], pieces=2)
    lasts = [cb[r.stop - 1:r.stop, :] for r in chunk_rows]
    cb_last = jnp.concatenate([jnp.broadcast_to(last, (CHUNK, dk)) for last in lasts], axis=0)
    eb = jnp.exp(cb)
    enb = jnp.exp(-cb)
    qf = q_ref[...].astype(F32) * scale
    kf = k_ref[...].astype(F32)
    q_e = (qf * eb).astype(BF16)
    q_n = (qf * enb).astype(BF16)
    k_n = (kf * enb).astype(BF16)
    k_e = (kf * eb).astype(BF16)
    k_in = (kf * jnp.exp(cb_last - cb)).astype(BF16)
    for h in range(heads):
        ks = slice(h * hk, (h + 1) * hk)
        vs = slice(h * hv, (h + 1) * hv)
        v = v_ref[:, vs]
        a_causal = lax.dot_general(q_e[:, ks], k_n[:, ks], NT_DIMS, preferred_element_type=F32)
        a_anti = lax.dot_general(q_n[:, ks], k_e[:, ks], NT_DIMS, preferred_element_type=F32)
        attn = jnp.where(same, jnp.where(causal, a_causal, a_anti), 0.0).astype(BF16)
        updates = [lax.dot_general(k_in[r, ks], v[r, :], TN_DIMS, preferred_element_type=F32) for r in chunk_rows]
        o = _dot(attn, v)
        state = state_ref[h]
        carried = []
        for c in range(nchunk):
            carried.append(state.astype(BF16))
            decay_col = jnp.broadcast_to(jnp.exp(lasts[c][:, ks]), (hk, hk)).T
            state = state * jnp.concatenate([decay_col] * (hv // hk), axis=1) + updates[c]
        state_ref[h] = state
        o = o + jnp.concatenate([_dot(q_e[r, ks], carried[c]) for c, r in enumerate(chunk_rows)], axis=0)
        ms = jnp.mean(o * o, axis=-1, keepdims=True)
        gate = _silu(g_ref[:, vs].astype(F32))
        y_ref[:, vs] = (o * lax.rsqrt(ms + EPS) * gate).astype(BF16)
    _residual_out(y_ref, w_ref, x_ref, gate_ref, fg_ref, o_ref, final)


def _gla_core(q, k, v, g, la, x, w_out, mod, layer, final_g, final, tl=256):
    b, s, dk = q.shape
    dv = v.shape[-1]
    d = x.shape[-1]
    heads = GLA_HEADS
    hk, hv = dk // heads, dv // heads
    _, modspec = _layer_vec_specs(layer, d)
    tok = lambda w: pl.BlockSpec((None, tl, w), lambda bi, l: (bi, l, 0))
    return pl.pallas_call(
        functools.partial(_gla_core_kernel, heads=heads, hk=hk, hv=hv, nchunk=tl // CHUNK, final=final),
        grid=(b, s // tl),
        in_specs=[tok(dk), tok(dk), tok(dv), tok(dv), tok(dk), tok(d), _resident((dv, d)), modspec(2),
                  _resident((1, d))],
        out_specs=tok(d),
        out_shape=jax.ShapeDtypeStruct((b, s, d), F32),
        scratch_shapes=[pltpu.VMEM((heads, hk, hv), F32), pltpu.VMEM((tl, dv), BF16)],
        compiler_params=_params("parallel", "arbitrary"),
        name="gla_core_out",
    )(q, k, v, g, la, x, w_out, mod, final_g)


def _out_kernel(y_ref, w_ref, x_ref, gate_ref, fg_ref, o_ref, *, final):
    _residual_out(y_ref, w_ref, x_ref, gate_ref, fg_ref, o_ref, final)


def _out_proj(y, w_out, x, mod, layer, final_g, final, tm=512):
    b, s, d = x.shape
    kdim = y.shape[-1]
    _, modspec = _layer_vec_specs(layer, d)
    tok = lambda w: pl.BlockSpec((None, tm, w), lambda bi, m: (bi, m, 0))
    return pl.pallas_call(
        functools.partial(_out_kernel, final=final),
        grid=(b, s // tm),
        in_specs=[tok(kdim), _resident((kdim, d)), tok(d), modspec(2), _resident((1, d))],
        out_specs=tok(d),
        out_shape=jax.ShapeDtypeStruct((b, s, d), F32),
        compiler_params=_params("parallel", "parallel"),
        name="out_proj_residual",
    )(y, w_out, x, mod, final_g)


def kernel(x, c, positions, mod_w, mod_b, norm_g, ret_w_in, ret_w_out, fox_w_in, fox_b_f, fox_q_gain,
           fox_k_gain, fox_w_out, gla_w_in, gla_w_gate2, gla_b_gate, gla_w_out, final_g):
    depth, d, _ = mod_w.shape
    b = x.shape[0]
    mod = _modulation(c, mod_w, mod_b).reshape(depth, b, 3, 1, d)
    ng = norm_g.astype(F32).reshape(depth, 1, d)
    fg = final_g.astype(F32).reshape(1, d)
    cos, sin = _rope_tables(positions, d // RET_HEADS // 2)
    for i in range(depth):
        j = i // N_MIXERS
        kind = i % N_MIXERS
        final = i == depth - 1
        if kind == 0:
            q, k, v, g = _ret_in_proj(x, cos, sin, ng, mod, i, ret_w_in[j].astype(BF16))
            x = _ret_core(q, k, v, g, x, ret_w_out[j].astype(BF16), mod, i, fg, final)
        elif kind == 1:
            q, k, v, g, cum = _fox_in_proj(x, ng, mod, i, fox_w_in[j].astype(BF16), fox_b_f[j])
            y = _fox_flash(q, k, v, g, cum, fox_q_gain[j], fox_k_gain[j])
            x = _out_proj(y, fox_w_out[j].astype(BF16), x, mod, i, fg, final)
        else:
            q, k, v, g, la = _gla_in_proj(x, ng, mod, i, gla_w_in[j].astype(BF16),
                                          gla_w_gate2[j].astype(BF16), gla_b_gate[j])
            x = _gla_core(q, k, v, g, la, x, gla_w_out[j].astype(BF16), mod, i, fg, final)
    return x
```

```python
import functools
import math

import jax
import jax.numpy as jnp
from jax import lax
from jax.experimental import pallas as pl
from jax.experimental.pallas import tpu as pltpu

F32 = jnp.float32
BF16 = jnp.bfloat16

EPS = 1e-6
N_MIXERS = 3
CHUNK = 64
RET_HEADS = 4
ROPE_BASE = 10000.0
FOX_HEADS = 16
GLA_HEADS = 4
GLA_RANK = 16
GLA_TAU = 16.0

LANES = 128
RET_SUPER = 256
NEG_BIG = -1e30
LOG2E = math.log2(math.e)
FOX_BOUND_MARGIN = 1.02
FOX_BOUND_MAX = 48.0
VMEM_LIMIT = 56 * 1024 * 1024

NT_DIMS = (((1,), (1,)), ((), ()))
TN_DIMS = (((0,), (0,)), ((), ()))


def _silu(x):
    return x * (1.0 / (1.0 + jnp.exp(-x)))


def _log_sigmoid(z):
    return jnp.minimum(z, 0.0) - jnp.log1p(jnp.exp(-jnp.abs(z)))


def _dot(a, b):
    return jnp.dot(a, b, preferred_element_type=F32)


def _split3(x):
    hi = x.astype(BF16)
    r1 = x - hi.astype(F32)
    mid = r1.astype(BF16)
    lo = (r1 - mid.astype(F32)).astype(BF16)
    return hi, mid, lo


def _tri_cumsum(tri, x, pieces=3):
    parts = _split3(x)[:pieces]
    out = _dot(tri, parts[0])
    for part in parts[1:]:
        out = out + _dot(tri, part)
    return out


def _lower_tri(n):
    row = lax.broadcasted_iota(jnp.int32, (n, n), 0)
    col = lax.broadcasted_iota(jnp.int32, (n, n), 1)
    return row >= col


def _norm_mod(x_ref, ng_ref, sc_ref, sh_ref):
    x = x_ref[...]
    ms = jnp.mean(x * x, axis=-1, keepdims=True)
    y = x * lax.rsqrt(ms + EPS) * ng_ref[...]
    return (y * (1.0 + sc_ref[...]) + sh_ref[...]).astype(BF16)


def _project(h, w_ref, col0, width, o_ref, tn):
    for n0 in range(0, width, tn):
        o_ref[:, n0:n0 + tn] = _dot(h, w_ref[:, col0 + n0:col0 + n0 + tn]).astype(o_ref.dtype)


def _params(*sem):
    return pltpu.CompilerParams(dimension_semantics=sem, vmem_limit_bytes=VMEM_LIMIT)


def _resident(shape):
    nd = len(shape)
    return pl.BlockSpec(shape, lambda *_: (0,) * nd, pipeline_mode=pl.Buffered(1))


def _mod_kernel(c_ref, w_ref, b_ref, o_ref):
    a = _silu(c_ref[...])
    o_ref[...] = jnp.dot(a, w_ref[...], preferred_element_type=F32,
                         precision=lax.Precision.HIGHEST) + b_ref[...]


def _modulation(c, mod_w, mod_b, tn=1024):
    depth, d, d3 = mod_w.shape
    b = c.shape[0]
    return pl.pallas_call(
        _mod_kernel,
        grid=(depth, d3 // tn),
        in_specs=[
            pl.BlockSpec((b, d), lambda i, n: (0, 0)),
            pl.BlockSpec((None, d, tn), lambda i, n: (i, 0, n)),
            pl.BlockSpec((None, 1, tn), lambda i, n: (i, 0, n)),
        ],
        out_specs=pl.BlockSpec((None, b, tn), lambda i, n: (i, 0, n)),
        out_shape=jax.ShapeDtypeStruct((depth, b, d3), F32),
        compiler_params=_params("parallel", "parallel"),
        name="adaln_modulation",
    )(c, mod_w, mod_b.reshape(depth, 1, d3))


def _layer_vec_specs(layer, d):
    ng = pl.BlockSpec((None, 1, d), lambda b, m: (layer, 0, 0))

    def mod(which):
        return pl.BlockSpec((None, None, None, 1, d), lambda b, m: (layer, b, which, 0, 0))
    return ng, mod


def _rope_kernel(pos_ref, inv_ref, cos_ref, sin_ref):
    ang = pos_ref[...].astype(F32) * inv_ref[...]
    cos_ref[...] = jnp.cos(ang)
    sin_ref[...] = jnp.sin(ang)


def _rope_tables(positions, half, tm=1024):
    b, s = positions.shape
    inv = (ROPE_BASE ** (-jnp.arange(half, dtype=F32) / half)).reshape(1, half)
    tok = lambda w: pl.BlockSpec((None, tm, w), lambda bi, m: (bi, m, 0))
    return pl.pallas_call(
        _rope_kernel,
        grid=(b, s // tm),
        in_specs=[tok(1), pl.BlockSpec((1, half), lambda bi, m: (0, 0))],
        out_specs=[tok(half), tok(half)],
        out_shape=[jax.ShapeDtypeStruct((b, s, half), F32)] * 2,
        compiler_params=_params("parallel", "parallel"),
        name="rope_tables",
    )(positions.reshape(b, s, 1), inv)


def _ret_in_kernel(x_ref, ng_ref, sc_ref, sh_ref, cos_ref, sin_ref, w_ref,
                   q_ref, k_ref, v_ref, g_ref, *, heads, hk, dv, tn):
    h = _norm_mod(x_ref, ng_ref, sc_ref, sh_ref)
    cos = cos_ref[...]
    sin = sin_ref[...]
    half = hk // 2
    dk = heads * hk
    _project(h, w_ref, 2 * dk, dv, v_ref, tn)
    for col0, o_ref, scale in ((0, q_ref, hk ** -0.5), (dk, k_ref, None)):
        for hd in range(heads):
            r = _dot(h, w_ref[:, col0 + hd * hk:col0 + (hd + 1) * hk])
            x1 = r[:, :half]
            x2 = r[:, half:]
            o1 = x1 * cos - x2 * sin
            o2 = x1 * sin + x2 * cos
            if scale is not None:
                o1 = o1 * scale
                o2 = o2 * scale
            o_ref[:, hd * hk:hd * hk + half] = o1.astype(BF16)
            o_ref[:, hd * hk + half:(hd + 1) * hk] = o2.astype(BF16)
    _project(h, w_ref, 2 * dk + dv, dv, g_ref, tn)


def _ret_in_proj(x, cos, sin, norm_g, mod, layer, w_in, tm=512, tn=512):
    b, s, d = x.shape
    heads = RET_HEADS
    dk = d
    dv = 2 * d
    hk = dk // heads
    n_in = w_in.shape[1]
    half = hk // 2
    ng, modspec = _layer_vec_specs(layer, d)
    tok = lambda w: pl.BlockSpec((None, tm, w), lambda bi, m: (bi, m, 0))
    return pl.pallas_call(
        functools.partial(_ret_in_kernel, heads=heads, hk=hk, dv=dv, tn=tn),
        grid=(b, s // tm),
        in_specs=[tok(d), ng, modspec(1), modspec(0), tok(half), tok(half), _resident((d, n_in))],
        out_specs=[tok(dk), tok(dk), tok(dv), tok(dv)],
        out_shape=[jax.ShapeDtypeStruct((b, s, w), BF16) for w in (dk, dk, dv, dv)],
        compiler_params=_params("parallel", "parallel"),
        name="retention_in_proj",
    )(x, norm_g, mod, mod, cos, sin, w_in)


def _residual_out(y_ref, w_ref, x_ref, gate_ref, fg_ref, o_ref, final):
    kdim = y_ref.shape[-1]
    half = kdim // 2
    r = _dot(y_ref[:, :half], w_ref[:half, :]) + _dot(y_ref[:, half:], w_ref[half:, :])
    xn = x_ref[...] + gate_ref[...] * r
    if final:
        ms = jnp.mean(xn * xn, axis=-1, keepdims=True)
        xn = xn * lax.rsqrt(ms + EPS) * fg_ref[...]
    o_ref[...] = xn


def _ret_core_kernel(q_ref, k_ref, v_ref, g_ref, x_ref, w_ref, gate_ref, fg_ref, o_ref,
                     state_ref, dmask_ref, qdec_ref, kdec_ref, y_ref, *, heads, hk, hv, sup, final):
    log_gamma = [math.log1p(-(2.0 ** (-5.0 - h))) for h in range(heads)]

    @pl.when(pl.program_id(1) == 0)
    def _():
        state_ref[...] = jnp.zeros_like(state_ref)
        t = lax.broadcasted_iota(jnp.int32, (sup, sup), 0)
        s = lax.broadcasted_iota(jnp.int32, (sup, sup), 1)
        dist = jnp.abs(t - s).astype(F32)
        visible = (s // CHUNK) <= (t // CHUNK)
        pos = lax.broadcasted_iota(jnp.int32, (sup, hk), 0).astype(F32)
        for h in range(heads):
            dmask_ref[h] = jnp.where(visible, jnp.exp(log_gamma[h] * dist), 0.0)
            qdec_ref[h] = jnp.exp(log_gamma[h] * (pos + 1.0))
            kdec_ref[h] = jnp.exp(log_gamma[h] * (sup - 1.0 - pos))

    for h in range(heads):
        q = q_ref[:, h * hk:(h + 1) * hk]
        k = k_ref[:, h * hk:(h + 1) * hk]
        v = v_ref[:, h * hv:(h + 1) * hv]
        scores = lax.dot_general(q, k, NT_DIMS, preferred_element_type=F32) * dmask_ref[h]
        state = state_ref[h]
        q_in = (q.astype(F32) * qdec_ref[h]).astype(BF16)
        o = _dot(scores.astype(BF16), v) + _dot(q_in, state.astype(BF16))
        k_in = (k.astype(F32) * kdec_ref[h]).astype(BF16)
        state_ref[h] = math.exp(log_gamma[h] * sup) * state + lax.dot_general(
            k_in, v, TN_DIMS, preferred_element_type=F32)
        mu = jnp.mean(o, axis=-1, keepdims=True)
        oc = o - mu
        var = jnp.mean(oc * oc, axis=-1, keepdims=True)
        gate = _silu(g_ref[:, h * hv:(h + 1) * hv].astype(F32))
        y_ref[:, h * hv:(h + 1) * hv] = (oc * lax.rsqrt(var + EPS) * gate).astype(BF16)
    _residual_out(y_ref, w_ref, x_ref, gate_ref, fg_ref, o_ref, final)


def _ret_core(q, k, v, g, x, w_out, mod, layer, final_g, final):
    b, s, dk = q.shape
    dv = v.shape[-1]
    d = x.shape[-1]
    heads = RET_HEADS
    hk, hv = dk // heads, dv // heads
    sup = RET_SUPER
    _, modspec = _layer_vec_specs(layer, d)
    tok = lambda w: pl.BlockSpec((None, sup, w), lambda bi, l: (bi, l, 0))
    return pl.pallas_call(
        functools.partial(_ret_core_kernel, heads=heads, hk=hk, hv=hv, sup=sup, final=final),
        grid=(b, s // sup),
        in_specs=[tok(dk), tok(dk), tok(dv), tok(dv), tok(d), _resident((dv, d)), modspec(2), _resident((1, d))],
        out_specs=tok(d),
        out_shape=jax.ShapeDtypeStruct((b, s, d), F32),
        scratch_shapes=[pltpu.VMEM((heads, hk, hv), F32), pltpu.VMEM((heads, sup, sup), F32),
                        pltpu.VMEM((heads, sup, hk), F32), pltpu.VMEM((heads, sup, hk), F32),
                        pltpu.VMEM((sup, dv), BF16)],
        compiler_params=_params("parallel", "arbitrary"),
        name="retention_core_out",
    )(q, k, v, g, x, w_out, mod, final_g)


def _fox_in_kernel(x_ref, ng_ref, sc_ref, sh_ref, w_ref, bf_ref,
                   q_ref, k_ref, v_ref, g_ref, cum_ref, carry_ref, *, width, tn):
    @pl.when(pl.program_id(1) == 0)
    def _():
        carry_ref[...] = jnp.zeros_like(carry_ref)

    h = _norm_mod(x_ref, ng_ref, sc_ref, sh_ref)
    log_f = _log_sigmoid(_dot(h, w_ref[:, 4 * width:]) + bf_ref[...])
    _project(h, w_ref, 0, width, q_ref, tn)
    rows = log_f.shape[0]
    tri = jnp.where(_lower_tri(rows), 1.0, 0.0).astype(BF16)
    cum = _tri_cumsum(tri, log_f) + carry_ref[...]
    carry_ref[...] = cum[rows - 1:rows, :]
    cum_ref[...] = cum
    for idx, o_ref in ((1, k_ref), (2, v_ref), (3, g_ref)):
        _project(h, w_ref, idx * width, width, o_ref, tn)


def _fox_in_proj(x, norm_g, mod, layer, w_in, b_f, tm=512, tn=512):
    b, s, d = x.shape
    heads = FOX_HEADS
    width = (w_in.shape[1] - heads) // 4
    w_pad = jnp.pad(w_in.astype(BF16), ((0, 0), (0, LANES - heads)))
    bias = jnp.pad(b_f.astype(F32), (0, LANES - heads)).reshape(1, LANES)
    ng, modspec = _layer_vec_specs(layer, d)
    tok = lambda w: pl.BlockSpec((None, tm, w), lambda bi, m: (bi, m, 0))
    return pl.pallas_call(
        functools.partial(_fox_in_kernel, width=width, tn=tn),
        grid=(b, s // tm),
        in_specs=[tok(d), ng, modspec(1), modspec(0),
                  _resident((d, 4 * width + LANES)), _resident((1, LANES))],
        out_specs=[tok(width)] * 4 + [tok(LANES)],
        out_shape=[jax.ShapeDtypeStruct((b, s, width), BF16)] * 4 + [jax.ShapeDtypeStruct((b, s, LANES), F32)],
        scratch_shapes=[pltpu.VMEM((1, LANES), F32)],
        compiler_params=_params("parallel", "arbitrary"),
        name="fox_in_proj",
    )(x, norm_g, mod, mod, w_pad, bias)


def _fox_flash_kernel(*refs, bq, hd, seq, bounded):
    if bounded:
        (q_ref, k_ref, v_ref, g_ref, cum_ref, qg_ref, kg_ref, bound_ref, y_ref,
         kx_ref, vx_ref, qx_ref, acc_ref) = refs
    else:
        (q_ref, k_ref, v_ref, g_ref, cum_ref, qg_ref, kg_ref, y_ref,
         kx_ref, vx_ref, qx_ref, acc_ref, m_ref) = refs
    pair = pl.program_id(1)
    qi = pl.program_id(2)
    lane = lax.broadcasted_iota(jnp.int32, (1, LANES), 1)
    first = lane < hd
    sels = (first, jnp.logical_not(first))
    aux0 = (hd, 0)

    def head_norm(x, gain):
        sq = x * x
        s0 = jnp.sum(jnp.where(first, sq, 0.0), axis=-1, keepdims=True)
        s1 = jnp.sum(jnp.where(first, 0.0, sq), axis=-1, keepdims=True)
        ms = jnp.where(first, s0, s1) * (1.0 / hd)
        return x * lax.rsqrt(ms + EPS) * gain

    def bias_lanes(cum, offset):
        src = lax.broadcasted_iota(jnp.int32, (LANES, LANES), 0)
        dst = lax.broadcasted_iota(jnp.int32, (LANES, LANES), 1)
        blocks = []
        for t in range(3):
            hit = jnp.logical_or(jnp.logical_and(src == 2 * pair, dst == aux0[0] + offset + t),
                                 jnp.logical_and(src == 2 * pair + 1, dst == aux0[1] + offset + t))
            blocks.append(jnp.where(hit, 1.0, 0.0))
        select = jnp.concatenate(blocks, axis=0).astype(BF16)
        moved = _dot(jnp.concatenate(_split3(cum), axis=1), select)
        out = []
        for j in range(2):
            one0 = aux0[j] + 3 - offset
            ones = jnp.where(jnp.logical_and(lane >= one0, lane < one0 + 3), 1.0, 0.0)
            out.append(moved + ones)
        return out

    @pl.when(qi == 0)
    def _():
        one = jnp.ones((), BF16)
        for r0 in range(0, seq, bq):
            kn = head_norm(k_ref[r0:r0 + bq, :].astype(F32), kg_ref[...])
            bias = bias_lanes(cum_ref[r0:r0 + bq, :] * (-LOG2E), 0)
            v = v_ref[r0:r0 + bq, :]
            for j in range(2):
                kx_ref[j, r0:r0 + bq, :] = jnp.where(sels[j], kn, bias[j]).astype(BF16)
                vx_ref[j, r0:r0 + bq, :] = jnp.where(sels[j], v, one)

    q0 = pl.multiple_of(qi * bq, bq)
    qn = head_norm(q_ref[...].astype(F32), qg_ref[...]) * (hd ** -0.5 * LOG2E)
    cum_q = cum_ref[pl.ds(q0, bq), :] * LOG2E
    if bounded:
        cum_q = cum_q - bound_ref[...]
    bias = bias_lanes(cum_q, 3)
    for j in range(2):
        qx_ref[j] = jnp.where(sels[j], qn, bias[j]).astype(BF16)
    acc_ref[...] = jnp.zeros_like(acc_ref)
    if not bounded:
        m_ref[...] = jnp.full_like(m_ref, NEG_BIG)
    causal = _lower_tri(bq)

    def bounded_blocks(full, diagonal):
        half = bq // 2
        work = []
        for kb in full:
            r0 = pl.multiple_of(kb * bq, bq)
            work += [(j, slice(0, bq), pl.ds(r0, bq), None) for j in range(2)]
        if diagonal:
            for j in range(2):
                work.append((j, slice(0, half), pl.ds(q0, half), causal[:half, :half]))
                work.append((j, slice(half, bq), pl.ds(q0, bq), causal[half:, :]))
        logits = [lax.dot_general(qx_ref[j, rq, :], kx_ref[j, rk, :], NT_DIMS, preferred_element_type=F32)
                  for j, rq, rk, _ in work]
        add = {}
        for (j, rq, rk, mask), s in zip(work, logits):
            if mask is not None:
                s = jnp.where(mask, s, NEG_BIG)
            pv = _dot(jnp.exp2(s).astype(BF16), vx_ref[j, rk, :])
            key = (j, rq.start, rq.stop)
            add[key] = pv if key not in add else add[key] + pv
        for (j, r0, r1), pv in add.items():
            acc_ref[j, r0:r1, :] += pv

    def step(kb, masked):
        r0 = pl.multiple_of(kb * bq, bq)
        logits = [lax.dot_general(qx_ref[j], kx_ref[j, pl.ds(r0, bq), :], NT_DIMS, preferred_element_type=F32)
                  for j in range(2)]
        for j in range(2):
            s = jnp.where(causal, logits[j], NEG_BIG) if masked else logits[j]
            m_prev = m_ref[j]
            m_new = jnp.maximum(m_prev, jnp.max(s, axis=-1, keepdims=True))
            p = jnp.exp2(s - jnp.concatenate([m_new] * (bq // LANES), axis=1))
            acc_ref[j] = jnp.exp2(m_prev - m_new) * acc_ref[j] + _dot(p.astype(BF16), vx_ref[j, pl.ds(r0, bq), :])
            m_ref[j] = m_new

    if bounded:
        def body(t, carry):
            bounded_blocks([2 * t, 2 * t + 1], False)
            return carry

        lax.fori_loop(0, jnp.right_shift(qi, 1), body, 0)
        odd = jnp.bitwise_and(qi, 1) == 1

        @pl.when(odd)
        def _():
            bounded_blocks([qi - 1], True)

        @pl.when(jnp.logical_not(odd))
        def _():
            bounded_blocks([], True)
    else:
        def body(kb, carry):
            step(kb, False)
            return carry

        lax.fori_loop(0, qi, body, 0)
        step(qi, True)
    num = jnp.where(first, acc_ref[0], acc_ref[1])
    den = jnp.where(first, pltpu.roll(acc_ref[0], hd, axis=1), pltpu.roll(acc_ref[1], hd, axis=1))
    y_ref[...] = (num / den * _silu(g_ref[...].astype(F32))).astype(BF16)


def _fox_flash(q, k, v, g, cum, q_gain, k_gain, bq=512):
    b, s, width = q.shape
    heads = FOX_HEADS
    hd = width // heads
    assert 2 * hd == LANES
    pairs = heads // 2
    qg = jnp.tile(q_gain.astype(F32), 2).reshape(1, LANES)
    kg = jnp.tile(k_gain.astype(F32), 2).reshape(1, LANES)
    blk = pl.BlockSpec((None, bq, LANES), lambda bi, p, i: (bi, i, p))
    full = pl.BlockSpec((None, s, LANES), lambda bi, p, i: (bi, 0, p))
    row = pl.BlockSpec((1, LANES), lambda bi, p, i: (0, 0))

    def call(bounded, *bound):
        return pl.pallas_call(
            functools.partial(_fox_flash_kernel, bq=bq, hd=hd, seq=s, bounded=bounded),
            grid=(b, pairs, s // bq),
            in_specs=[blk, full, full, blk, pl.BlockSpec((None, s, LANES), lambda bi, p, i: (bi, 0, 0)), row, row]
                     + [row] * len(bound),
            out_specs=blk,
            out_shape=jax.ShapeDtypeStruct((b, s, width), BF16),
            scratch_shapes=[pltpu.VMEM((2, s, LANES), BF16), pltpu.VMEM((2, s, LANES), BF16),
                            pltpu.VMEM((2, bq, LANES), BF16), pltpu.VMEM((2, bq, LANES), F32)]
                           + ([] if bounded else [pltpu.VMEM((2, bq, LANES), F32)]),
            compiler_params=_params("parallel", "parallel", "arbitrary"),
            name="fox_flash_bounded" if bounded else "fox_flash_online",
        )(q, k, v, g, cum, qg, kg, *bound)

    bound = FOX_BOUND_MARGIN * LOG2E * hd ** 0.5 * jnp.max(jnp.abs(qg)) * jnp.max(jnp.abs(kg))
    return lax.cond(bound <= FOX_BOUND_MAX,
                    lambda: call(True, jnp.full((1, LANES), bound, F32)),
                    lambda: call(False))


def _gla_in_kernel(x_ref, ng_ref, sc_ref, sh_ref, w_ref, wg2_ref, bg_ref,
                   q_ref, k_ref, v_ref, g_ref, la_ref, *, dk, dv, tn):
    h = _norm_mod(x_ref, ng_ref, sc_ref, sh_ref)
    main = 2 * dk + 2 * dv
    r = _dot(h, w_ref[:, main:])
    _project(h, w_ref, 0, dk, q_ref, tn)
    z = _dot(r.astype(BF16), wg2_ref[...]) + bg_ref[...]
    _project(h, w_ref, dk, dk, k_ref, tn)
    la_ref[...] = _log_sigmoid(z) * (1.0 / GLA_TAU)
    _project(h, w_ref, 2 * dk, dv, v_ref, tn)
    _project(h, w_ref, 2 * dk + dv, dv, g_ref, tn)


def _gla_in_proj(x, norm_g, mod, layer, w_in, w_gate2, b_gate, tm=512, tn=512):
    b, s, d = x.shape
    dk, dv, rank = d // 2, d, GLA_RANK
    main = 2 * dk + 2 * dv
    w_pad = jnp.pad(w_in.astype(BF16), ((0, 0), (0, LANES - rank)))
    w_g2 = jnp.pad(w_gate2.astype(BF16), ((0, LANES - rank), (0, 0)))
    ng, modspec = _layer_vec_specs(layer, d)
    tok = lambda w, : pl.BlockSpec((None, tm, w), lambda bi, m: (bi, m, 0))
    return pl.pallas_call(
        functools.partial(_gla_in_kernel, dk=dk, dv=dv, tn=tn),
        grid=(b, s // tm),
        in_specs=[tok(d), ng, modspec(1), modspec(0), _resident((d, main + LANES)),
                  _resident((LANES, dk)), _resident((1, dk))],
        out_specs=[tok(dk), tok(dk), tok(dv), tok(dv), tok(dk)],
        out_shape=[jax.ShapeDtypeStruct((b, s, w), BF16) for w in (dk, dk, dv, dv)]
                  + [jax.ShapeDtypeStruct((b, s, dk), F32)],
        compiler_params=_params("parallel", "parallel"),
        name="gla_in_proj",
    )(x, norm_g, mod, mod, w_pad, w_g2, b_gate.astype(F32).reshape(1, dk))


def _gla_core_kernel(q_ref, k_ref, v_ref, g_ref, la_ref, x_ref, w_ref, gate_ref, fg_ref, o_ref,
                     state_ref, y_ref, *, heads, hk, hv, nchunk, final):
    @pl.when(pl.program_id(1) == 0)
    def _():
        state_ref[...] = jnp.zeros_like(state_ref)

    tile = nchunk * CHUNK
    row = lax.broadcasted_iota(jnp.int32, (tile, tile), 0)
    col = lax.broadcasted_iota(jnp.int32, (tile, tile), 1)
    same = (row // CHUNK) == (col // CHUNK)
    causal = row >= col
    tri = jnp.where(jnp.logical_and(same, causal), 1.0, 0.0).astype(BF16)
    scale = hk ** -0.5
    dk = heads * hk
    chunk_rows = [slice(c * CHUNK, (c + 1) * CHUNK) for c in range(nchunk)]

    cb = _tri_cumsum(tri, la_ref[...], pieces=2)
    lasts = [cb[r.stop - 1:r.stop, :] for r in chunk_rows]
    cb_last = jnp.concatenate([jnp.broadcast_to(last, (CHUNK, dk)) for last in lasts], axis=0)
    eb = jnp.exp(cb)
    enb = jnp.exp(-cb)
    qf = q_ref[...].astype(F32) * scale
    kf = k_ref[...].astype(F32)
    q_e = (qf * eb).astype(BF16)
    q_n = (qf * enb).astype(BF16)
    k_n = (kf * enb).astype(BF16)
    k_e = (kf * eb).astype(BF16)
    k_in = (kf * jnp.exp(cb_last - cb)).astype(BF16)
    for h in range(heads):
        ks = slice(h * hk, (h + 1) * hk)
        vs = slice(h * hv, (h + 1) * hv)
        v = v_ref[:, vs]
        a_causal = lax.dot_general(q_e[:, ks], k_n[:, ks], NT_DIMS, preferred_element_type=F32)
        a_anti = lax.dot_general(q_n[:, ks], k_e[:, ks], NT_DIMS, preferred_element_type=F32)
        attn = jnp.where(same, jnp.where(causal, a_causal, a_anti), 0.0).astype(BF16)
        updates = [lax.dot_general(k_in[r, ks], v[r, :], TN_DIMS, preferred_element_type=F32) for r in chunk_rows]
        o = _dot(attn, v)
        state = state_ref[h]
        carried = []
        for c in range(nchunk):
            carried.append(state.astype(BF16))
            decay_col = jnp.broadcast_to(jnp.exp(lasts[c][:, ks]), (hk, hk)).T
            state = state * jnp.concatenate([decay_col] * (hv // hk), axis=1) + updates[c]
        state_ref[h] = state
        o = o + jnp.concatenate([_dot(q_e[r, ks], carried[c]) for c, r in enumerate(chunk_rows)], axis=0)
        ms = jnp.mean(o * o, axis=-1, keepdims=True)
        gate = _silu(g_ref[:, vs].astype(F32))
        y_ref[:, vs] = (o * lax.rsqrt(ms + EPS) * gate).astype(BF16)
    _residual_out(y_ref, w_ref, x_ref, gate_ref, fg_ref, o_ref, final)


def _gla_core(q, k, v, g, la, x, w_out, mod, layer, final_g, final, tl=256):
    b, s, dk = q.shape
    dv = v.shape[-1]
    d = x.shape[-1]
    heads = GLA_HEADS
    hk, hv = dk // heads, dv // heads
    _, modspec = _layer_vec_specs(layer, d)
    tok = lambda w: pl.BlockSpec((None, tl, w), lambda bi, l: (bi, l, 0))
    return pl.pallas_call(
        functools.partial(_gla_core_kernel, heads=heads, hk=hk, hv=hv, nchunk=tl // CHUNK, final=final),
        grid=(b, s // tl),
        in_specs=[tok(dk), tok(dk), tok(dv), tok(dv), tok(dk), tok(d), _resident((dv, d)), modspec(2),
                  _resident((1, d))],
        out_specs=tok(d),
        out_shape=jax.ShapeDtypeStruct((b, s, d), F32),
        scratch_shapes=[pltpu.VMEM((heads, hk, hv), F32), pltpu.VMEM((tl, dv), BF16)],
        compiler_params=_params("parallel", "arbitrary"),
        name="gla_core_out",
    )(q, k, v, g, la, x, w_out, mod, final_g)


def _out_kernel(y_ref, w_ref, x_ref, gate_ref, fg_ref, o_ref, *, final):
    _residual_out(y_ref, w_ref, x_ref, gate_ref, fg_ref, o_ref, final)


def _out_proj(y, w_out, x, mod, layer, final_g, final, tm=512):
    b, s, d = x.shape
    kdim = y.shape[-1]
    _, modspec = _layer_vec_specs(layer, d)
    tok = lambda w: pl.BlockSpec((None, tm, w), lambda bi, m: (bi, m, 0))
    return pl.pallas_call(
        functools.partial(_out_kernel, final=final),
        grid=(b, s // tm),
        in_specs=[tok(kdim), _resident((kdim, d)), tok(d), modspec(2), _resident((1, d))],
        out_specs=tok(d),
        out_shape=jax.ShapeDtypeStruct((b, s, d), F32),
        compiler_params=_params("parallel", "parallel"),
        name="out_proj_residual",
    )(y, w_out, x, mod, final_g)


def kernel(x, c, positions, mod_w, mod_b, norm_g, ret_w_in, ret_w_out, fox_w_in, fox_b_f, fox_q_gain,
           fox_k_gain, fox_w_out, gla_w_in, gla_w_gate2, gla_b_gate, gla_w_out, final_g):
    depth, d, _ = mod_w.shape
    b = x.shape[0]
    mod = _modulation(c, mod_w, mod_b).reshape(depth, b, 3, 1, d)
    ng = norm_g.astype(F32).reshape(depth, 1, d)
    fg = final_g.astype(F32).reshape(1, d)
    cos, sin = _rope_tables(positions, d // RET_HEADS // 2)
    for i in range(depth):
        j = i // N_MIXERS
        kind = i % N_MIXERS
        final = i == depth - 1
        if kind == 0:
            q, k, v, g = _ret_in_proj(x, cos, sin, ng, mod, i, ret_w_in[j].astype(BF16))
            x = _ret_core(q, k, v, g, x, ret_w_out[j].astype(BF16), mod, i, fg, final)
        elif kind == 1:
            q, k, v, g, cum = _fox_in_proj(x, ng, mod, i, fox_w_in[j], fox_b_f[j])
            y = _fox_flash(q, k, v, g, cum, fox_q_gain[j], fox_k_gain[j])
            x = _out_proj(y, fox_w_out[j].astype(BF16), x, mod, i, fg, final)
        else:
            q, k, v, g, la = _gla_in_proj(x, ng, mod, i, gla_w_in[j], gla_w_gate2[j], gla_b_gate[j])
            x = _gla_core(q, k, v, g, la, x, gla_w_out[j].astype(BF16), mod, i, fg, final)
    return x
```

```python
import functools
import math

import jax
import jax.numpy as jnp
from jax import lax
from jax.experimental import pallas as pl
from jax.experimental.pallas import tpu as pltpu

F32 = jnp.float32
BF16 = jnp.bfloat16

EPS = 1e-6
N_MIXERS = 3
CHUNK = 64
RET_HEADS = 4
ROPE_BASE = 10000.0
FOX_HEADS = 16
GLA_HEADS = 4
GLA_RANK = 16
GLA_TAU = 16.0

LANES = 128
RET_SUPER = 256
NEG_BIG = -1e30
LOG2E = math.log2(math.e)
FOX_BOUND_MARGIN = 1.02
FOX_BOUND_MAX = 48.0
FOX_SKIP_BITS = 160.0
VMEM_LIMIT = 56 * 1024 * 1024

NT_DIMS = (((1,), (1,)), ((), ()))
TN_DIMS = (((0,), (0,)), ((), ()))


def _silu(x):
    return x * (1.0 / (1.0 + jnp.exp(-x)))


def _log_sigmoid(z):
    return jnp.minimum(z, 0.0) - jnp.log1p(jnp.exp(-jnp.abs(z)))


def _dot(a, b):
    return jnp.dot(a, b, preferred_element_type=F32)


def _split3(x):
    hi = x.astype(BF16)
    r1 = x - hi.astype(F32)
    mid = r1.astype(BF16)
    lo = (r1 - mid.astype(F32)).astype(BF16)
    return hi, mid, lo


def _tri_cumsum(tri, x, pieces=3):
    parts = _split3(x)[:pieces]
    out = _dot(tri, parts[0])
    for part in parts[1:]:
        out = out + _dot(tri, part)
    return out


def _lower_tri(n):
    row = lax.broadcasted_iota(jnp.int32, (n, n), 0)
    col = lax.broadcasted_iota(jnp.int32, (n, n), 1)
    return row >= col


def _norm_mod(x_ref, ng_ref, sc_ref, sh_ref):
    x = x_ref[...]
    ms = jnp.mean(x * x, axis=-1, keepdims=True)
    y = x * lax.rsqrt(ms + EPS) * ng_ref[...]
    return (y * (1.0 + sc_ref[...]) + sh_ref[...]).astype(BF16)


def _project(h, w_ref, col0, width, o_ref, tn):
    for n0 in range(0, width, tn):
        o_ref[:, n0:n0 + tn] = _dot(h, w_ref[:, col0 + n0:col0 + n0 + tn]).astype(o_ref.dtype)


def _params(*sem):
    return pltpu.CompilerParams(dimension_semantics=sem, vmem_limit_bytes=VMEM_LIMIT)


def _resident(shape, layer=None):
    nd = len(shape)
    if layer is None:
        return pl.BlockSpec(shape, lambda *_: (0,) * nd, pipeline_mode=pl.Buffered(1))
    return pl.BlockSpec((None,) + tuple(shape), lambda *_: (layer,) + (0,) * nd, pipeline_mode=pl.Buffered(1))


def _mod_kernel(c_ref, w_ref, b_ref, o_ref):
    a = _silu(c_ref[...])
    o_ref[...] = jnp.dot(a, w_ref[...], preferred_element_type=F32,
                         precision=lax.Precision.HIGHEST) + b_ref[...]


def _modulation(c, mod_w, mod_b, tn=1024):
    depth, d, d3 = mod_w.shape
    b = c.shape[0]
    return pl.pallas_call(
        _mod_kernel,
        grid=(depth, d3 // tn),
        in_specs=[
            pl.BlockSpec((b, d), lambda i, n: (0, 0)),
            pl.BlockSpec((None, d, tn), lambda i, n: (i, 0, n)),
            pl.BlockSpec((None, 1, tn), lambda i, n: (i, 0, n)),
        ],
        out_specs=pl.BlockSpec((None, b, tn), lambda i, n: (i, 0, n)),
        out_shape=jax.ShapeDtypeStruct((depth, b, d3), F32),
        compiler_params=_params("parallel", "parallel"),
        name="adaln_modulation",
    )(c, mod_w, mod_b.reshape(depth, 1, d3))


def _layer_vec_specs(layer, d):
    ng = pl.BlockSpec((None, 1, d), lambda b, m: (layer, 0, 0))

    def mod(which):
        return pl.BlockSpec((None, None, None, 1, d), lambda b, m: (layer, b, which, 0, 0))
    return ng, mod


def _rope_kernel(pos_ref, inv_ref, cos_ref, sin_ref):
    ang = pos_ref[...].astype(F32) * inv_ref[...]
    cos_ref[...] = jnp.cos(ang)
    sin_ref[...] = jnp.sin(ang)


def _rope_tables(positions, half, tm=1024):
    b, s = positions.shape
    inv = (ROPE_BASE ** (-jnp.arange(half, dtype=F32) / half)).reshape(1, half)
    tok = lambda w: pl.BlockSpec((None, tm, w), lambda bi, m: (bi, m, 0))
    return pl.pallas_call(
        _rope_kernel,
        grid=(b, s // tm),
        in_specs=[tok(1), pl.BlockSpec((1, half), lambda bi, m: (0, 0))],
        out_specs=[tok(half), tok(half)],
        out_shape=[jax.ShapeDtypeStruct((b, s, half), F32)] * 2,
        compiler_params=_params("parallel", "parallel"),
        name="rope_tables",
    )(positions.reshape(b, s, 1), inv)


def _ret_in_kernel(x_ref, ng_ref, sc_ref, sh_ref, cos_ref, sin_ref, w_ref,
                   q_ref, k_ref, v_ref, g_ref, *, heads, hk, dv, tn):
    h = _norm_mod(x_ref, ng_ref, sc_ref, sh_ref)
    cos = cos_ref[...]
    sin = sin_ref[...]
    half = hk // 2
    dk = heads * hk
    _project(h, w_ref, 2 * dk, dv, v_ref, tn)
    for col0, o_ref, scale in ((0, q_ref, hk ** -0.5), (dk, k_ref, None)):
        for hd in range(heads):
            r = _dot(h, w_ref[:, col0 + hd * hk:col0 + (hd + 1) * hk])
            x1 = r[:, :half]
            x2 = r[:, half:]
            o1 = x1 * cos - x2 * sin
            o2 = x1 * sin + x2 * cos
            if scale is not None:
                o1 = o1 * scale
                o2 = o2 * scale
            o_ref[:, hd * hk:hd * hk + half] = o1.astype(BF16)
            o_ref[:, hd * hk + half:(hd + 1) * hk] = o2.astype(BF16)
    _project(h, w_ref, 2 * dk + dv, dv, g_ref, tn)


def _ret_in_proj(x, cos, sin, norm_g, mod, layer, w_in, j, tm=512, tn=512):
    b, s, d = x.shape
    heads = RET_HEADS
    dk = d
    dv = 2 * d
    hk = dk // heads
    n_in = w_in.shape[-1]
    half = hk // 2
    ng, modspec = _layer_vec_specs(layer, d)
    tok = lambda w: pl.BlockSpec((None, tm, w), lambda bi, m: (bi, m, 0))
    return pl.pallas_call(
        functools.partial(_ret_in_kernel, heads=heads, hk=hk, dv=dv, tn=tn),
        grid=(b, s // tm),
        in_specs=[tok(d), ng, modspec(1), modspec(0), tok(half), tok(half), _resident((d, n_in), j)],
        out_specs=[tok(dk), tok(dk), tok(dv), tok(dv)],
        out_shape=[jax.ShapeDtypeStruct((b, s, w), BF16) for w in (dk, dk, dv, dv)],
        compiler_params=_params("parallel", "parallel"),
        name="retention_in_proj",
    )(x, norm_g, mod, mod, cos, sin, w_in)


def _residual_out(y_ref, w_ref, x_ref, gate_ref, fg_ref, o_ref, final):
    kdim = y_ref.shape[-1]
    half = kdim // 2
    r = _dot(y_ref[:, :half], w_ref[:half, :]) + _dot(y_ref[:, half:], w_ref[half:, :])
    xn = x_ref[...] + gate_ref[...] * r
    if final:
        ms = jnp.mean(xn * xn, axis=-1, keepdims=True)
        xn = xn * lax.rsqrt(ms + EPS) * fg_ref[...]
    o_ref[...] = xn


def _ret_core_kernel(q_ref, k_ref, v_ref, g_ref, x_ref, w_ref, gate_ref, fg_ref, o_ref,
                     state_ref, dmask_ref, qdec_ref, kdec_ref, y_ref, *, heads, hk, hv, sup, final):
    log_gamma = [math.log1p(-(2.0 ** (-5.0 - h))) for h in range(heads)]

    @pl.when(pl.program_id(1) == 0)
    def _():
        state_ref[...] = jnp.zeros_like(state_ref)
        t = lax.broadcasted_iota(jnp.int32, (sup, sup), 0)
        s = lax.broadcasted_iota(jnp.int32, (sup, sup), 1)
        dist = jnp.abs(t - s).astype(F32)
        visible = (s // CHUNK) <= (t // CHUNK)
        pos = lax.broadcasted_iota(jnp.int32, (sup, hk), 0).astype(F32)
        for h in range(heads):
            dmask_ref[h] = jnp.where(visible, jnp.exp(log_gamma[h] * dist), 0.0)
            qdec_ref[h] = jnp.exp(log_gamma[h] * (pos + 1.0))
            kdec_ref[h] = jnp.exp(log_gamma[h] * (sup - 1.0 - pos))

    for h in range(heads):
        q = q_ref[:, h * hk:(h + 1) * hk]
        k = k_ref[:, h * hk:(h + 1) * hk]
        v = v_ref[:, h * hv:(h + 1) * hv]
        scores = lax.dot_general(q, k, NT_DIMS, preferred_element_type=F32) * dmask_ref[h]
        state = state_ref[h]
        q_in = (q.astype(F32) * qdec_ref[h]).astype(BF16)
        o = _dot(scores.astype(BF16), v) + _dot(q_in, state.astype(BF16))
        k_in = (k.astype(F32) * kdec_ref[h]).astype(BF16)
        state_ref[h] = math.exp(log_gamma[h] * sup) * state + lax.dot_general(
            k_in, v, TN_DIMS, preferred_element_type=F32)
        mu = jnp.mean(o, axis=-1, keepdims=True)
        oc = o - mu
        var = jnp.mean(oc * oc, axis=-1, keepdims=True)
        gate = _silu(g_ref[:, h * hv:(h + 1) * hv].astype(F32))
        y_ref[:, h * hv:(h + 1) * hv] = (oc * lax.rsqrt(var + EPS) * gate).astype(BF16)
    _residual_out(y_ref, w_ref, x_ref, gate_ref, fg_ref, o_ref, final)


def _ret_core(q, k, v, g, x, w_out, j, mod, layer, final_g, final):
    b, s, dk = q.shape
    dv = v.shape[-1]
    d = x.shape[-1]
    heads = RET_HEADS
    hk, hv = dk // heads, dv // heads
    sup = RET_SUPER
    _, modspec = _layer_vec_specs(layer, d)
    tok = lambda w: pl.BlockSpec((None, sup, w), lambda bi, l: (bi, l, 0))
    return pl.pallas_call(
        functools.partial(_ret_core_kernel, heads=heads, hk=hk, hv=hv, sup=sup, final=final),
        grid=(b, s // sup),
        in_specs=[tok(dk), tok(dk), tok(dv), tok(dv), tok(d), _resident((dv, d), j), modspec(2), _resident((1, d))],
        out_specs=tok(d),
        out_shape=jax.ShapeDtypeStruct((b, s, d), F32),
        scratch_shapes=[pltpu.VMEM((heads, hk, hv), F32), pltpu.VMEM((heads, sup, sup), F32),
                        pltpu.VMEM((heads, sup, hk), F32), pltpu.VMEM((heads, sup, hk), F32),
                        pltpu.VMEM((sup, dv), BF16)],
        compiler_params=_params("parallel", "arbitrary"),
        name="retention_core_out",
    )(q, k, v, g, x, w_out, mod, final_g)


def _fox_in_kernel(x_ref, ng_ref, sc_ref, sh_ref, w_ref, bf_ref,
                   q_ref, k_ref, v_ref, g_ref, cum_ref, carry_ref, *, width, tn):
    @pl.when(pl.program_id(1) == 0)
    def _():
        carry_ref[...] = jnp.zeros_like(carry_ref)

    h = _norm_mod(x_ref, ng_ref, sc_ref, sh_ref)
    log_f = _log_sigmoid(_dot(h, w_ref[:, 4 * width:]) + bf_ref[...])
    _project(h, w_ref, 0, width, q_ref, tn)
    rows = log_f.shape[0]
    tri = jnp.where(_lower_tri(rows), 1.0, 0.0).astype(BF16)
    cum = _tri_cumsum(tri, log_f) + carry_ref[...]
    carry_ref[...] = cum[rows - 1:rows, :]
    cum_ref[...] = cum
    for idx, o_ref in ((1, k_ref), (2, v_ref), (3, g_ref)):
        _project(h, w_ref, idx * width, width, o_ref, tn)


def _fox_in_proj(x, norm_g, mod, layer, w_in, b_f, tm=512, tn=512):
    b, s, d = x.shape
    heads = FOX_HEADS
    width = (w_in.shape[1] - heads) // 4
    w_pad = jnp.pad(w_in.astype(BF16), ((0, 0), (0, LANES - heads)))
    bias = jnp.pad(b_f.astype(F32), (0, LANES - heads)).reshape(1, LANES)
    ng, modspec = _layer_vec_specs(layer, d)
    tok = lambda w: pl.BlockSpec((None, tm, w), lambda bi, m: (bi, m, 0))
    return pl.pallas_call(
        functools.partial(_fox_in_kernel, width=width, tn=tn),
        grid=(b, s // tm),
        in_specs=[tok(d), ng, modspec(1), modspec(0),
                  _resident((d, 4 * width + LANES)), _resident((1, LANES))],
        out_specs=[tok(width)] * 4 + [tok(LANES)],
        out_shape=[jax.ShapeDtypeStruct((b, s, width), BF16)] * 4 + [jax.ShapeDtypeStruct((b, s, LANES), F32)],
        scratch_shapes=[pltpu.VMEM((1, LANES), F32)],
        compiler_params=_params("parallel", "arbitrary"),
        name="fox_in_proj",
    )(x, norm_g, mod, mod, w_pad, bias)


def _fox_flash_kernel(*refs, bq, hd, seq, bounded):
    if bounded:
        (start_ref, q_ref, k_ref, v_ref, g_ref, cum_ref, qg_ref, kg_ref, bound_ref, y_ref,
         kx_ref, vx_ref, qx_ref, acc_ref) = refs
    else:
        (q_ref, k_ref, v_ref, g_ref, cum_ref, qg_ref, kg_ref, y_ref,
         kx_ref, vx_ref, qx_ref, acc_ref, m_ref) = refs
    pair = pl.program_id(1)
    qi = pl.program_id(2)
    lane = lax.broadcasted_iota(jnp.int32, (1, LANES), 1)
    first = lane < hd
    sels = (first, jnp.logical_not(first))
    aux0 = (hd, 0)

    def head_norm(x, gain):
        sq = x * x
        s0 = jnp.sum(jnp.where(first, sq, 0.0), axis=-1, keepdims=True)
        s1 = jnp.sum(jnp.where(first, 0.0, sq), axis=-1, keepdims=True)
        ms = jnp.where(first, s0, s1) * (1.0 / hd)
        return x * lax.rsqrt(ms + EPS) * gain

    def bias_lanes(cum, offset):
        src = lax.broadcasted_iota(jnp.int32, (LANES, LANES), 0)
        dst = lax.broadcasted_iota(jnp.int32, (LANES, LANES), 1)
        blocks = []
        for t in range(3):
            hit = jnp.logical_or(jnp.logical_and(src == 2 * pair, dst == aux0[0] + offset + t),
                                 jnp.logical_and(src == 2 * pair + 1, dst == aux0[1] + offset + t))
            blocks.append(jnp.where(hit, 1.0, 0.0))
        select = jnp.concatenate(blocks, axis=0).astype(BF16)
        moved = _dot(jnp.concatenate(_split3(cum), axis=1), select)
        out = []
        for j in range(2):
            one0 = aux0[j] + 3 - offset
            ones = jnp.where(jnp.logical_and(lane >= one0, lane < one0 + 3), 1.0, 0.0)
            out.append(moved + ones)
        return out

    @pl.when(qi == 0)
    def _():
        one = jnp.ones((), BF16)
        for r0 in range(0, seq, bq):
            kn = head_norm(k_ref[r0:r0 + bq, :].astype(F32), kg_ref[...])
            bias = bias_lanes(cum_ref[r0:r0 + bq, :] * (-LOG2E), 0)
            v = v_ref[r0:r0 + bq, :]
            for j in range(2):
                kx_ref[j, r0:r0 + bq, :] = jnp.where(sels[j], kn, bias[j]).astype(BF16)
                vx_ref[j, r0:r0 + bq, :] = jnp.where(sels[j], v, one)

    q0 = pl.multiple_of(qi * bq, bq)
    qn = head_norm(q_ref[...].astype(F32), qg_ref[...]) * (hd ** -0.5 * LOG2E)
    cum_q = cum_ref[pl.ds(q0, bq), :] * LOG2E
    if bounded:
        cum_q = cum_q - bound_ref[...]
    bias = bias_lanes(cum_q, 3)
    for j in range(2):
        qx_ref[j] = jnp.where(sels[j], qn, bias[j]).astype(BF16)
    acc_ref[...] = jnp.zeros_like(acc_ref)
    if not bounded:
        m_ref[...] = jnp.full_like(m_ref, NEG_BIG)
    causal = _lower_tri(bq)

    def bounded_blocks(full, diagonal):
        half = bq // 2
        work = []
        for kb in full:
            r0 = pl.multiple_of(kb * bq, bq)
            work += [(j, slice(0, bq), pl.ds(r0, bq), None) for j in range(2)]
        if diagonal:
            for j in range(2):
                work.append((j, slice(0, half), pl.ds(q0, half), causal[:half, :half]))
                work.append((j, slice(half, bq), pl.ds(q0, bq), causal[half:, :]))
        logits = [lax.dot_general(qx_ref[j, rq, :], kx_ref[j, rk, :], NT_DIMS, preferred_element_type=F32)
                  for j, rq, rk, _ in work]
        add = {}
        for (j, rq, rk, mask), s in zip(work, logits):
            if mask is not None:
                s = jnp.where(mask, s, NEG_BIG)
            pv = _dot(jnp.exp2(s).astype(BF16), vx_ref[j, rk, :])
            key = (j, rq.start, rq.stop)
            add[key] = pv if key not in add else add[key] + pv
        for (j, r0, r1), pv in add.items():
            acc_ref[j, r0:r1, :] += pv

    def step(kb, masked):
        r0 = pl.multiple_of(kb * bq, bq)
        logits = [lax.dot_general(qx_ref[j], kx_ref[j, pl.ds(r0, bq), :], NT_DIMS, preferred_element_type=F32)
                  for j in range(2)]
        for j in range(2):
            s = jnp.where(causal, logits[j], NEG_BIG) if masked else logits[j]
            m_prev = m_ref[j]
            m_new = jnp.maximum(m_prev, jnp.max(s, axis=-1, keepdims=True))
            p = jnp.exp2(s - jnp.concatenate([m_new] * (bq // LANES), axis=1))
            acc_ref[j] = jnp.exp2(m_prev - m_new) * acc_ref[j] + _dot(p.astype(BF16), vx_ref[j, pl.ds(r0, bq), :])
            m_ref[j] = m_new

    if bounded:
        nq = seq // bq
        start = start_ref[(pl.program_id(0) * pl.num_programs(1) + pair) * nq + qi]
        count = qi - start

        def body(t, carry):
            bounded_blocks([start + 2 * t, start + 2 * t + 1], False)
            return carry

        lax.fori_loop(0, jnp.right_shift(count, 1), body, 0)
        odd = jnp.bitwise_and(count, 1) == 1

        @pl.when(odd)
        def _():
            bounded_blocks([qi - 1], True)

        @pl.when(jnp.logical_not(odd))
        def _():
            bounded_blocks([], True)
    else:
        def body(kb, carry):
            step(kb, False)
            return carry

        lax.fori_loop(0, qi, body, 0)
        step(qi, True)
    num = jnp.where(first, acc_ref[0], acc_ref[1])
    den = jnp.where(first, pltpu.roll(acc_ref[0], hd, axis=1), pltpu.roll(acc_ref[1], hd, axis=1))
    y_ref[...] = (num / den * _silu(g_ref[...].astype(F32))).astype(BF16)


def _fox_flash(q, k, v, g, cum, q_gain, k_gain, bq=512):
    b, s, width = q.shape
    heads = FOX_HEADS
    hd = width // heads
    assert 2 * hd == LANES
    pairs = heads // 2
    qg = jnp.tile(q_gain.astype(F32), 2).reshape(1, LANES)
    kg = jnp.tile(k_gain.astype(F32), 2).reshape(1, LANES)
    nq = s // bq
    blk = pl.BlockSpec((None, bq, LANES), lambda bi, p, i, *_: (bi, i, p))
    full = pl.BlockSpec((None, s, LANES), lambda bi, p, i, *_: (bi, 0, p))
    row = pl.BlockSpec((1, LANES), lambda bi, p, i, *_: (0, 0))

    def call(bounded, *extra):
        grid_spec = pltpu.PrefetchScalarGridSpec(
            num_scalar_prefetch=1 if bounded else 0,
            grid=(b, pairs, nq),
            in_specs=[blk, full, full, blk, pl.BlockSpec((None, s, LANES), lambda bi, p, i, *_: (bi, 0, 0)),
                      row, row] + ([row] if bounded else []),
            out_specs=blk,
            scratch_shapes=[pltpu.VMEM((2, s, LANES), BF16), pltpu.VMEM((2, s, LANES), BF16),
                            pltpu.VMEM((2, bq, LANES), BF16), pltpu.VMEM((2, bq, LANES), F32)]
                           + ([] if bounded else [pltpu.VMEM((2, bq, LANES), F32)]))
        return pl.pallas_call(
            functools.partial(_fox_flash_kernel, bq=bq, hd=hd, seq=s, bounded=bounded),
            grid_spec=grid_spec,
            out_shape=jax.ShapeDtypeStruct((b, s, width), BF16),
            compiler_params=_params("parallel", "parallel", "arbitrary"),
            name="fox_flash_bounded" if bounded else "fox_flash_online",
        )(*extra[:1], q, k, v, g, cum, qg, kg, *extra[1:])

    def first_key_blocks():
        at_first = cum[:, ::bq, :heads]
        at_last = cum[:, bq - 1::bq, :heads]
        gap = at_first[:, :, None, :] - at_last[:, None, :, :]
        blocks = jnp.arange(nq, dtype=jnp.int32)
        earlier = (blocks[None, :] < blocks[:, None])[None, :, :, None]
        needed = jnp.logical_and(gap * LOG2E > -FOX_SKIP_BITS, earlier)
        first = jnp.where(jnp.any(needed, axis=2), jnp.argmax(needed, axis=2).astype(jnp.int32),
                          blocks[None, :, None])
        first = jnp.min(first.reshape(b, nq, pairs, 2), axis=-1)
        return jnp.transpose(first, (0, 2, 1)).reshape(-1)

    bound = FOX_BOUND_MARGIN * LOG2E * hd ** 0.5 * jnp.max(jnp.abs(qg)) * jnp.max(jnp.abs(kg))
    return lax.cond(bound <= FOX_BOUND_MAX,
                    lambda: call(True, first_key_blocks(), jnp.full((1, LANES), bound, F32)),
                    lambda: call(False))


def _gla_in_kernel(x_ref, ng_ref, sc_ref, sh_ref, w_ref, wg2_ref, bg_ref,
                   q_ref, k_ref, v_ref, g_ref, la_ref, *, dk, dv, tn):
    h = _norm_mod(x_ref, ng_ref, sc_ref, sh_ref)
    main = 2 * dk + 2 * dv
    r = _dot(h, w_ref[:, main:])
    _project(h, w_ref, 0, dk, q_ref, tn)
    z = _dot(r.astype(BF16), wg2_ref[...]) + bg_ref[...]
    _project(h, w_ref, dk, dk, k_ref, tn)
    la_ref[...] = _log_sigmoid(z) * (1.0 / GLA_TAU)
    _project(h, w_ref, 2 * dk, dv, v_ref, tn)
    _project(h, w_ref, 2 * dk + dv, dv, g_ref, tn)


def _gla_in_proj(x, norm_g, mod, layer, w_in, w_gate2, b_gate, tm=512, tn=512):
    b, s, d = x.shape
    dk, dv, rank = d // 2, d, GLA_RANK
    main = 2 * dk + 2 * dv
    w_pad = jnp.pad(w_in.astype(BF16), ((0, 0), (0, LANES - rank)))
    w_g2 = jnp.pad(w_gate2.astype(BF16), ((0, LANES - rank), (0, 0)))
    ng, modspec = _layer_vec_specs(layer, d)
    tok = lambda w, : pl.BlockSpec((None, tm, w), lambda bi, m: (bi, m, 0))
    return pl.pallas_call(
        functools.partial(_gla_in_kernel, dk=dk, dv=dv, tn=tn),
        grid=(b, s // tm),
        in_specs=[tok(d), ng, modspec(1), modspec(0), _resident((d, main + LANES)),
                  _resident((LANES, dk)), _resident((1, dk))],
        out_specs=[tok(dk), tok(dk), tok(dv), tok(dv), tok(dk)],
        out_shape=[jax.ShapeDtypeStruct((b, s, w), BF16) for w in (dk, dk, dv, dv)]
                  + [jax.ShapeDtypeStruct((b, s, dk), F32)],
        compiler_params=_params("parallel", "parallel"),
        name="gla_in_proj",
    )(x, norm_g, mod, mod, w_pad, w_g2, b_gate.astype(F32).reshape(1, dk))


def _gla_core_kernel(q_ref, k_ref, v_ref, g_ref, la_ref, x_ref, w_ref, gate_ref, fg_ref, o_ref,
                     state_ref, y_ref, *, heads, hk, hv, nchunk, final):
    @pl.when(pl.program_id(1) == 0)
    def _():
        state_ref[...] = jnp.zeros_like(state_ref)

    tile = nchunk * CHUNK
    row = lax.broadcasted_iota(jnp.int32, (tile, tile), 0)
    col = lax.broadcasted_iota(jnp.int32, (tile, tile), 1)
    same = (row // CHUNK) == (col // CHUNK)
    causal = row >= col
    tri = jnp.where(jnp.logical_and(same, causal), 1.0, 0.0).astype(BF16)
    scale = hk ** -0.5
    dk = heads * hk
    chunk_rows = [slice(c * CHUNK, (c + 1) * CHUNK) for c in range(nchunk)]

    cb = _tri_cumsum(tri, la_ref[...], pieces=2)
    lasts = [cb[r.stop - 1:r.stop, :] for r in chunk_rows]
    cb_last = jnp.concatenate([jnp.broadcast_to(last, (CHUNK, dk)) for last in lasts], axis=0)
    eb = jnp.exp(cb)
    enb = jnp.exp(-cb)
    qf = q_ref[...].astype(F32) * scale
    kf = k_ref[...].astype(F32)
    q_e = (qf * eb).astype(BF16)
    q_n = (qf * enb).astype(BF16)
    k_n = (kf * enb).astype(BF16)
    k_e = (kf * eb).astype(BF16)
    k_in = (kf * jnp.exp(cb_last - cb)).astype(BF16)
    for h in range(heads):
        ks = slice(h * hk, (h + 1) * hk)
        vs = slice(h * hv, (h + 1) * hv)
        v = v_ref[:, vs]
        a_causal = lax.dot_general(q_e[:, ks], k_n[:, ks], NT_DIMS, preferred_element_type=F32)
        a_anti = lax.dot_general(q_n[:, ks], k_e[:, ks], NT_DIMS, preferred_element_type=F32)
        attn = jnp.where(same, jnp.where(causal, a_causal, a_anti), 0.0).astype(BF16)
        updates = [lax.dot_general(k_in[r, ks], v[r, :], TN_DIMS, preferred_element_type=F32) for r in chunk_rows]
        o = _dot(attn, v)
        state = state_ref[h]
        carried = []
        for c in range(nchunk):
            carried.append(state.astype(BF16))
            decay_col = jnp.broadcast_to(jnp.exp(lasts[c][:, ks]), (hk, hk)).T
            state = state * jnp.concatenate([decay_col] * (hv // hk), axis=1) + updates[c]
        state_ref[h] = state
        o = o + jnp.concatenate([_dot(q_e[r, ks], carried[c]) for c, r in enumerate(chunk_rows)], axis=0)
        ms = jnp.mean(o * o, axis=-1, keepdims=True)
        gate = _silu(g_ref[:, vs].astype(F32))
        y_ref[:, vs] = (o * lax.rsqrt(ms + EPS) * gate).astype(BF16)
    _residual_out(y_ref, w_ref, x_ref, gate_ref, fg_ref, o_ref, final)


def _gla_core(q, k, v, g, la, x, w_out, mod, layer, final_g, final, tl=256):
    b, s, dk = q.shape
    dv = v.shape[-1]
    d = x.shape[-1]
    heads = GLA_HEADS
    hk, hv = dk // heads, dv // heads
    _, modspec = _layer_vec_specs(layer, d)
    tok = lambda w: pl.BlockSpec((None, tl, w), lambda bi, l: (bi, l, 0))
    return pl.pallas_call(
        functools.partial(_gla_core_kernel, heads=heads, hk=hk, hv=hv, nchunk=tl // CHUNK, final=final),
        grid=(b, s // tl),
        in_specs=[tok(dk), tok(dk), tok(dv), tok(dv), tok(dk), tok(d), _resident((dv, d)), modspec(2),
                  _resident((1, d))],
        out_specs=tok(d),
        out_shape=jax.ShapeDtypeStruct((b, s, d), F32),
        scratch_shapes=[pltpu.VMEM((heads, hk, hv), F32), pltpu.VMEM((tl, dv), BF16)],
        compiler_params=_params("parallel", "arbitrary"),
        name="gla_core_out",
    )(q, k, v, g, la, x, w_out, mod, final_g)


def _out_kernel(y_ref, w_ref, x_ref, gate_ref, fg_ref, o_ref, *, final):
    _residual_out(y_ref, w_ref, x_ref, gate_ref, fg_ref, o_ref, final)


def _out_proj(y, w_out, x, mod, layer, final_g, final, tm=512):
    b, s, d = x.shape
    kdim = y.shape[-1]
    _, modspec = _layer_vec_specs(layer, d)
    tok = lambda w: pl.BlockSpec((None, tm, w), lambda bi, m: (bi, m, 0))
    return pl.pallas_call(
        functools.partial(_out_kernel, final=final),
        grid=(b, s // tm),
        in_specs=[tok(kdim), _resident((kdim, d)), tok(d), modspec(2), _resident((1, d))],
        out_specs=tok(d),
        out_shape=jax.ShapeDtypeStruct((b, s, d), F32),
        compiler_params=_params("parallel", "parallel"),
        name="out_proj_residual",
    )(y, w_out, x, mod, final_g)


def kernel(x, c, positions, mod_w, mod_b, norm_g, ret_w_in, ret_w_out, fox_w_in, fox_b_f, fox_q_gain,
           fox_k_gain, fox_w_out, gla_w_in, gla_w_gate2, gla_b_gate, gla_w_out, final_g):
    depth, d, _ = mod_w.shape
    b = x.shape[0]
    mod = _modulation(c, mod_w, mod_b).reshape(depth, b, 3, 1, d)
    ng = norm_g.astype(F32).reshape(depth, 1, d)
    fg = final_g.astype(F32).reshape(1, d)
    cos, sin = _rope_tables(positions, d // RET_HEADS // 2)
    ret_w_in_bf = ret_w_in.astype(BF16)
    ret_w_out_bf = ret_w_out.astype(BF16)
    for i in range(depth):
        j = i // N_MIXERS
        kind = i % N_MIXERS
        final = i == depth - 1
        if kind == 0:
            q, k, v, g = _ret_in_proj(x, cos, sin, ng, mod, i, ret_w_in_bf, j)
            x = _ret_core(q, k, v, g, x, ret_w_out_bf, j, mod, i, fg, final)
        elif kind == 1:
            q, k, v, g, cum = _fox_in_proj(x, ng, mod, i, fox_w_in[j], fox_b_f[j])
            y = _fox_flash(q, k, v, g, cum, fox_q_gain[j], fox_k_gain[j])
            x = _out_proj(y, fox_w_out[j].astype(BF16), x, mod, i, fg, final)
        else:
            q, k, v, g, la = _gla_in_proj(x, ng, mod, i, gla_w_in[j], gla_w_gate2[j], gla_b_gate[j])
            x = _gla_core(q, k, v, g, la, x, gla_w_out[j].astype(BF16), mod, i, fg, final)
    return x
```

```python
import functools
import math

import jax
import jax.numpy as jnp
from jax import lax
from jax.experimental import pallas as pl
from jax.experimental.pallas import tpu as pltpu

F32 = jnp.float32
BF16 = jnp.bfloat16

EPS = 1e-6
N_MIXERS = 3
CHUNK = 64
RET_HEADS = 4
ROPE_BASE = 10000.0
FOX_HEADS = 16
GLA_HEADS = 4
GLA_RANK = 16
GLA_TAU = 16.0

LANES = 128
RET_SUPER = 256
NEG_BIG = -1e30
LOG2E = math.log2(math.e)
FOX_BOUND_MARGIN = 1.02
FOX_BOUND_MAX = 48.0
FOX_SKIP_BITS = 160.0
VMEM_LIMIT = 56 * 1024 * 1024

NT_DIMS = (((1,), (1,)), ((), ()))
TN_DIMS = (((0,), (0,)), ((), ()))


def _silu(x):
    return x * (1.0 / (1.0 + jnp.exp(-x)))


def _log_sigmoid(z):
    return jnp.minimum(z, 0.0) - jnp.log1p(jnp.exp(-jnp.abs(z)))


def _dot(a, b):
    return jnp.dot(a, b, preferred_element_type=F32)


def _split3(x):
    hi = x.astype(BF16)
    r1 = x - hi.astype(F32)
    mid = r1.astype(BF16)
    lo = (r1 - mid.astype(F32)).astype(BF16)
    return hi, mid, lo


def _tri_cumsum(tri, x, pieces=3):
    parts = _split3(x)[:pieces]
    out = _dot(tri, parts[0])
    for part in parts[1:]:
        out = out + _dot(tri, part)
    return out


def _lower_tri(n):
    row = lax.broadcasted_iota(jnp.int32, (n, n), 0)
    col = lax.broadcasted_iota(jnp.int32, (n, n), 1)
    return row >= col


def _norm_mod(x_ref, ng_ref, sc_ref, sh_ref):
    x = x_ref[...]
    ms = jnp.mean(x * x, axis=-1, keepdims=True)
    y = x * lax.rsqrt(ms + EPS) * ng_ref[...]
    return (y * (1.0 + sc_ref[...]) + sh_ref[...]).astype(BF16)


def _project(h, w_ref, col0, width, o_ref, tn):
    for n0 in range(0, width, tn):
        o_ref[:, n0:n0 + tn] = _dot(h, w_ref[:, col0 + n0:col0 + n0 + tn]).astype(o_ref.dtype)


def _cast_widen(w, extra):
    return jnp.concatenate([w.astype(BF16), jnp.zeros((w.shape[0], extra), BF16)], axis=1)


def _params(*sem):
    return pltpu.CompilerParams(dimension_semantics=sem, vmem_limit_bytes=VMEM_LIMIT)


def _resident(shape, layer=None):
    nd = len(shape)
    if layer is None:
        return pl.BlockSpec(shape, lambda *_: (0,) * nd, pipeline_mode=pl.Buffered(1))
    return pl.BlockSpec((None,) + tuple(shape), lambda *_: (layer,) + (0,) * nd, pipeline_mode=pl.Buffered(1))


def _mod_kernel(c_ref, w_ref, b_ref, o_ref):
    a = _silu(c_ref[...])
    o_ref[...] = jnp.dot(a, w_ref[...], preferred_element_type=F32,
                         precision=lax.Precision.HIGHEST) + b_ref[...]


def _modulation(c, mod_w, mod_b, tn=1024):
    depth, d, d3 = mod_w.shape
    b = c.shape[0]
    return pl.pallas_call(
        _mod_kernel,
        grid=(depth, d3 // tn),
        in_specs=[
            pl.BlockSpec((b, d), lambda i, n: (0, 0)),
            pl.BlockSpec((None, d, tn), lambda i, n: (i, 0, n)),
            pl.BlockSpec((None, 1, tn), lambda i, n: (i, 0, n)),
        ],
        out_specs=pl.BlockSpec((None, b, tn), lambda i, n: (i, 0, n)),
        out_shape=jax.ShapeDtypeStruct((depth, b, d3), F32),
        compiler_params=_params("parallel", "parallel"),
        name="adaln_modulation",
    )(c, mod_w, mod_b.reshape(depth, 1, d3))


def _layer_vec_specs(layer, d):
    ng = pl.BlockSpec((None, 1, d), lambda b, m: (layer, 0, 0))

    def mod(which):
        return pl.BlockSpec((None, None, None, 1, d), lambda b, m: (layer, b, which, 0, 0))
    return ng, mod


def _rope_kernel(pos_ref, inv_ref, cos_ref, sin_ref):
    ang = pos_ref[...].astype(F32) * inv_ref[...]
    cos_ref[...] = jnp.cos(ang)
    sin_ref[...] = jnp.sin(ang)


def _rope_tables(positions, half, tm=1024):
    b, s = positions.shape
    inv = (ROPE_BASE ** (-jnp.arange(half, dtype=F32) / half)).reshape(1, half)
    tok = lambda w: pl.BlockSpec((None, tm, w), lambda bi, m: (bi, m, 0))
    return pl.pallas_call(
        _rope_kernel,
        grid=(b, s // tm),
        in_specs=[tok(1), pl.BlockSpec((1, half), lambda bi, m: (0, 0))],
        out_specs=[tok(half), tok(half)],
        out_shape=[jax.ShapeDtypeStruct((b, s, half), F32)] * 2,
        compiler_params=_params("parallel", "parallel"),
        name="rope_tables",
    )(positions.reshape(b, s, 1), inv)


def _ret_in_kernel(x_ref, ng_ref, sc_ref, sh_ref, cos_ref, sin_ref, w_ref,
                   q_ref, k_ref, v_ref, g_ref, *, heads, hk, dv, tn):
    h = _norm_mod(x_ref, ng_ref, sc_ref, sh_ref)
    cos = cos_ref[...]
    sin = sin_ref[...]
    half = hk // 2
    dk = heads * hk
    _project(h, w_ref, 2 * dk, dv, v_ref, tn)
    for col0, o_ref, scale in ((0, q_ref, hk ** -0.5), (dk, k_ref, None)):
        for hd in range(heads):
            r = _dot(h, w_ref[:, col0 + hd * hk:col0 + (hd + 1) * hk])
            x1 = r[:, :half]
            x2 = r[:, half:]
            o1 = x1 * cos - x2 * sin
            o2 = x1 * sin + x2 * cos
            if scale is not None:
                o1 = o1 * scale
                o2 = o2 * scale
            o_ref[:, hd * hk:hd * hk + half] = o1.astype(BF16)
            o_ref[:, hd * hk + half:(hd + 1) * hk] = o2.astype(BF16)
    _project(h, w_ref, 2 * dk + dv, dv, g_ref, tn)


def _ret_in_proj(x, cos, sin, norm_g, mod, layer, w_in, j, tm=512, tn=512):
    b, s, d = x.shape
    heads = RET_HEADS
    dk = d
    dv = 2 * d
    hk = dk // heads
    n_in = w_in.shape[-1]
    half = hk // 2
    ng, modspec = _layer_vec_specs(layer, d)
    tok = lambda w: pl.BlockSpec((None, tm, w), lambda bi, m: (bi, m, 0))
    return pl.pallas_call(
        functools.partial(_ret_in_kernel, heads=heads, hk=hk, dv=dv, tn=tn),
        grid=(b, s // tm),
        in_specs=[tok(d), ng, modspec(1), modspec(0), tok(half), tok(half), _resident((d, n_in), j)],
        out_specs=[tok(dk), tok(dk), tok(dv), tok(dv)],
        out_shape=[jax.ShapeDtypeStruct((b, s, w), BF16) for w in (dk, dk, dv, dv)],
        compiler_params=_params("parallel", "parallel"),
        name="retention_in_proj",
    )(x, norm_g, mod, mod, cos, sin, w_in)


def _residual_out(y_ref, w_ref, x_ref, gate_ref, fg_ref, o_ref, final):
    kdim = y_ref.shape[-1]
    half = kdim // 2
    r = _dot(y_ref[:, :half], w_ref[:half, :]) + _dot(y_ref[:, half:], w_ref[half:, :])
    xn = x_ref[...] + gate_ref[...] * r
    if final:
        ms = jnp.mean(xn * xn, axis=-1, keepdims=True)
        xn = xn * lax.rsqrt(ms + EPS) * fg_ref[...]
    o_ref[...] = xn


def _ret_core_kernel(q_ref, k_ref, v_ref, g_ref, x_ref, w_ref, gate_ref, fg_ref, o_ref,
                     state_ref, dmask_ref, qdec_ref, kdec_ref, y_ref, *, heads, hk, hv, sup, final):
    log_gamma = [math.log1p(-(2.0 ** (-5.0 - h))) for h in range(heads)]

    @pl.when(pl.program_id(1) == 0)
    def _():
        state_ref[...] = jnp.zeros_like(state_ref)
        t = lax.broadcasted_iota(jnp.int32, (sup, sup), 0)
        s = lax.broadcasted_iota(jnp.int32, (sup, sup), 1)
        dist = jnp.abs(t - s).astype(F32)
        visible = (s // CHUNK) <= (t // CHUNK)
        pos = lax.broadcasted_iota(jnp.int32, (sup, hk), 0).astype(F32)
        for h in range(heads):
            dmask_ref[h] = jnp.where(visible, jnp.exp(log_gamma[h] * dist), 0.0)
            qdec_ref[h] = jnp.exp(log_gamma[h] * (pos + 1.0))
            kdec_ref[h] = jnp.exp(log_gamma[h] * (sup - 1.0 - pos))

    for r0 in range(0, q_ref.shape[0], sup):
        rows = slice(r0, r0 + sup)
        for h in range(heads):
            q = q_ref[rows, h * hk:(h + 1) * hk]
            k = k_ref[rows, h * hk:(h + 1) * hk]
            v = v_ref[rows, h * hv:(h + 1) * hv]
            scores = lax.dot_general(q, k, NT_DIMS, preferred_element_type=F32) * dmask_ref[h]
            state = state_ref[h]
            q_in = (q.astype(F32) * qdec_ref[h]).astype(BF16)
            o = _dot(scores.astype(BF16), v) + _dot(q_in, state.astype(BF16))
            k_in = (k.astype(F32) * kdec_ref[h]).astype(BF16)
            state_ref[h] = math.exp(log_gamma[h] * sup) * state + lax.dot_general(
                k_in, v, TN_DIMS, preferred_element_type=F32)
            mu = jnp.mean(o, axis=-1, keepdims=True)
            oc = o - mu
            var = jnp.mean(oc * oc, axis=-1, keepdims=True)
            gate = _silu(g_ref[rows, h * hv:(h + 1) * hv].astype(F32))
            y_ref[rows, h * hv:(h + 1) * hv] = (oc * lax.rsqrt(var + EPS) * gate).astype(BF16)
    _residual_out(y_ref, w_ref, x_ref, gate_ref, fg_ref, o_ref, final)


def _ret_core(q, k, v, g, x, w_out, j, mod, layer, final_g, final, tl=2 * RET_SUPER):
    b, s, dk = q.shape
    dv = v.shape[-1]
    d = x.shape[-1]
    heads = RET_HEADS
    hk, hv = dk // heads, dv // heads
    sup = RET_SUPER
    _, modspec = _layer_vec_specs(layer, d)
    tok = lambda w: pl.BlockSpec((None, tl, w), lambda bi, l: (bi, l, 0))
    return pl.pallas_call(
        functools.partial(_ret_core_kernel, heads=heads, hk=hk, hv=hv, sup=sup, final=final),
        grid=(b, s // tl),
        in_specs=[tok(dk), tok(dk), tok(dv), tok(dv), tok(d), _resident((dv, d), j), modspec(2), _resident((1, d))],
        out_specs=tok(d),
        out_shape=jax.ShapeDtypeStruct((b, s, d), F32),
        scratch_shapes=[pltpu.VMEM((heads, hk, hv), F32), pltpu.VMEM((heads, sup, sup), F32),
                        pltpu.VMEM((heads, sup, hk), F32), pltpu.VMEM((heads, sup, hk), F32),
                        pltpu.VMEM((tl, dv), BF16)],
        compiler_params=_params("parallel", "arbitrary"),
        name="retention_core_out",
    )(q, k, v, g, x, w_out, mod, final_g)


def _fox_in_kernel(x_ref, ng_ref, sc_ref, sh_ref, w_ref, bf_ref,
                   q_ref, k_ref, v_ref, g_ref, cum_ref, carry_ref, *, width, tn):
    @pl.when(pl.program_id(1) == 0)
    def _():
        carry_ref[...] = jnp.zeros_like(carry_ref)

    h = _norm_mod(x_ref, ng_ref, sc_ref, sh_ref)
    log_f = _log_sigmoid(_dot(h, w_ref[:, 4 * width:]) + bf_ref[...])
    _project(h, w_ref, 0, width, q_ref, tn)
    rows = log_f.shape[0]
    tri = jnp.where(_lower_tri(rows), 1.0, 0.0).astype(BF16)
    cum = _tri_cumsum(tri, log_f) + carry_ref[...]
    carry_ref[...] = cum[rows - 1:rows, :]
    cum_ref[...] = cum
    for idx, o_ref in ((1, k_ref), (2, v_ref), (3, g_ref)):
        _project(h, w_ref, idx * width, width, o_ref, tn)


def _fox_in_proj(x, norm_g, mod, layer, w_in, b_f, tm=512, tn=512):
    b, s, d = x.shape
    heads = FOX_HEADS
    width = (w_in.shape[1] - heads) // 4
    w_pad = _cast_widen(w_in, LANES - heads)
    bias = jnp.pad(b_f.astype(F32), (0, LANES - heads)).reshape(1, LANES)
    ng, modspec = _layer_vec_specs(layer, d)
    tok = lambda w: pl.BlockSpec((None, tm, w), lambda bi, m: (bi, m, 0))
    return pl.pallas_call(
        functools.partial(_fox_in_kernel, width=width, tn=tn),
        grid=(b, s // tm),
        in_specs=[tok(d), ng, modspec(1), modspec(0),
                  _resident((d, 4 * width + LANES)), _resident((1, LANES))],
        out_specs=[tok(width)] * 4 + [tok(LANES)],
        out_shape=[jax.ShapeDtypeStruct((b, s, width), BF16)] * 4 + [jax.ShapeDtypeStruct((b, s, LANES), F32)],
        scratch_shapes=[pltpu.VMEM((1, LANES), F32)],
        compiler_params=_params("parallel", "arbitrary"),
        name="fox_in_proj",
    )(x, norm_g, mod, mod, w_pad, bias)


def _fox_flash_kernel(*refs, bq, hd, seq, bounded):
    if bounded:
        (start_ref, q_ref, k_ref, v_ref, g_ref, cum_ref, qg_ref, kg_ref, bound_ref, y_ref,
         kx_ref, vx_ref, qx_ref, acc_ref) = refs
    else:
        (q_ref, k_ref, v_ref, g_ref, cum_ref, qg_ref, kg_ref, y_ref,
         kx_ref, vx_ref, qx_ref, acc_ref, m_ref) = refs
    pair = pl.program_id(1)
    lane = lax.broadcasted_iota(jnp.int32, (1, LANES), 1)
    first = lane < hd
    sels = (first, jnp.logical_not(first))
    aux0 = (hd, 0)

    def head_norm(x, gain):
        sq = x * x
        s0 = jnp.sum(jnp.where(first, sq, 0.0), axis=-1, keepdims=True)
        s1 = jnp.sum(jnp.where(first, 0.0, sq), axis=-1, keepdims=True)
        ms = jnp.where(first, s0, s1) * (1.0 / hd)
        return x * lax.rsqrt(ms + EPS) * gain

    def bias_lanes(cum, offset):
        src = lax.broadcasted_iota(jnp.int32, (LANES, LANES), 0)
        dst = lax.broadcasted_iota(jnp.int32, (LANES, LANES), 1)
        blocks = []
        for t in range(3):
            hit = jnp.logical_or(jnp.logical_and(src == 2 * pair, dst == aux0[0] + offset + t),
                                 jnp.logical_and(src == 2 * pair + 1, dst == aux0[1] + offset + t))
            blocks.append(jnp.where(hit, 1.0, 0.0))
        select = jnp.concatenate(blocks, axis=0).astype(BF16)
        moved = _dot(jnp.concatenate(_split3(cum), axis=1), select)
        out = []
        for j in range(2):
            one0 = aux0[j] + 3 - offset
            ones = jnp.where(jnp.logical_and(lane >= one0, lane < one0 + 3), 1.0, 0.0)
            out.append(moved + ones)
        return out

    one = jnp.ones((), BF16)
    for r0 in range(0, seq, bq):
        kn = head_norm(k_ref[r0:r0 + bq, :].astype(F32), kg_ref[...])
        bias = bias_lanes(cum_ref[r0:r0 + bq, :] * (-LOG2E), 0)
        v = v_ref[r0:r0 + bq, :]
        for j in range(2):
            kx_ref[j, r0:r0 + bq, :] = jnp.where(sels[j], kn, bias[j]).astype(BF16)
            vx_ref[j, r0:r0 + bq, :] = jnp.where(sels[j], v, one)

    def query_block(qi, carry):
        q0 = pl.multiple_of(qi * bq, bq)
        qrows = pl.ds(q0, bq)
        qn = head_norm(q_ref[qrows, :].astype(F32), qg_ref[...]) * (hd ** -0.5 * LOG2E)
        cum_q = cum_ref[qrows, :] * LOG2E
        if bounded:
            cum_q = cum_q - bound_ref[...]
        bias = bias_lanes(cum_q, 3)
        for j in range(2):
            qx_ref[j] = jnp.where(sels[j], qn, bias[j]).astype(BF16)
        acc_ref[...] = jnp.zeros_like(acc_ref)
        if not bounded:
            m_ref[...] = jnp.full_like(m_ref, NEG_BIG)
        causal = _lower_tri(bq)

        def bounded_blocks(full, diagonal):
            half = bq // 2
            work = []
            for kb in full:
                r0 = pl.multiple_of(kb * bq, bq)
                work += [(j, slice(0, bq), pl.ds(r0, bq), None) for j in range(2)]
            if diagonal:
                for j in range(2):
                    work.append((j, slice(0, half), pl.ds(q0, half), causal[:half, :half]))
                    work.append((j, slice(half, bq), qrows, causal[half:, :]))
            logits = [lax.dot_general(qx_ref[j, rq, :], kx_ref[j, rk, :], NT_DIMS, preferred_element_type=F32)
                      for j, rq, rk, _ in work]
            add = {}
            for (j, rq, rk, mask), s in zip(work, logits):
                if mask is not None:
                    s = jnp.where(mask, s, NEG_BIG)
                pv = _dot(jnp.exp2(s).astype(BF16), vx_ref[j, rk, :])
                key = (j, rq.start, rq.stop)
                add[key] = pv if key not in add else add[key] + pv
            for (j, r0, r1), pv in add.items():
                acc_ref[j, r0:r1, :] += pv

        def step(kb, masked):
            rk = pl.ds(pl.multiple_of(kb * bq, bq), bq)
            logits = [lax.dot_general(qx_ref[j], kx_ref[j, rk, :], NT_DIMS, preferred_element_type=F32)
                      for j in range(2)]
            for j in range(2):
                s = jnp.where(causal, logits[j], NEG_BIG) if masked else logits[j]
                m_prev = m_ref[j]
                m_new = jnp.maximum(m_prev, jnp.max(s, axis=-1, keepdims=True))
                p = jnp.exp2(s - jnp.concatenate([m_new] * (bq // LANES), axis=1))
                acc_ref[j] = jnp.exp2(m_prev - m_new) * acc_ref[j] + _dot(p.astype(BF16), vx_ref[j, rk, :])
                m_ref[j] = m_new

        if bounded:
            start = start_ref[(pl.program_id(0) * pl.num_programs(1) + pair) * (seq // bq) + qi]
            count = qi - start

            def body(t, carry):
                bounded_blocks([start + 2 * t, start + 2 * t + 1], False)
                return carry

            lax.fori_loop(0, jnp.right_shift(count, 1), body, 0)
            odd = jnp.bitwise_and(count, 1) == 1

            @pl.when(odd)
            def _():
                bounded_blocks([qi - 1], True)

            @pl.when(jnp.logical_not(odd))
            def _():
                bounded_blocks([], True)
        else:
            def body(kb, carry):
                step(kb, False)
                return carry

            lax.fori_loop(0, qi, body, 0)
            step(qi, True)
        num = jnp.where(first, acc_ref[0], acc_ref[1])
        den = jnp.where(first, pltpu.roll(acc_ref[0], hd, axis=1), pltpu.roll(acc_ref[1], hd, axis=1))
        y_ref[qrows, :] = (num / den * _silu(g_ref[qrows, :].astype(F32))).astype(BF16)
        return carry

    lax.fori_loop(0, seq // bq, query_block, 0)


def _fox_flash(q, k, v, g, cum, q_gain, k_gain, bq=512):
    b, s, width = q.shape
    heads = FOX_HEADS
    hd = width // heads
    assert 2 * hd == LANES
    pairs = heads // 2
    qg = jnp.tile(q_gain.astype(F32), 2).reshape(1, LANES)
    kg = jnp.tile(k_gain.astype(F32), 2).reshape(1, LANES)
    nq = s // bq
    full = pl.BlockSpec((None, s, LANES), lambda bi, p, *_: (bi, 0, p))
    row = pl.BlockSpec((1, LANES), lambda bi, p, *_: (0, 0))

    def call(bounded, *extra):
        grid_spec = pltpu.PrefetchScalarGridSpec(
            num_scalar_prefetch=1 if bounded else 0,
            grid=(b, pairs),
            in_specs=[full, full, full, full, pl.BlockSpec((None, s, LANES), lambda bi, p, *_: (bi, 0, 0)),
                      row, row] + ([row] if bounded else []),
            out_specs=full,
            scratch_shapes=[pltpu.VMEM((2, s, LANES), BF16), pltpu.VMEM((2, s, LANES), BF16),
                            pltpu.VMEM((2, bq, LANES), BF16), pltpu.VMEM((2, bq, LANES), F32)]
                           + ([] if bounded else [pltpu.VMEM((2, bq, LANES), F32)]))
        return pl.pallas_call(
            functools.partial(_fox_flash_kernel, bq=bq, hd=hd, seq=s, bounded=bounded),
            grid_spec=grid_spec,
            out_shape=jax.ShapeDtypeStruct((b, s, width), BF16),
            compiler_params=_params("parallel", "parallel"),
            name="fox_flash_bounded" if bounded else "fox_flash_online",
        )(*extra[:1], q, k, v, g, cum, qg, kg, *extra[1:])

    def first_key_blocks():
        at_first = cum[:, ::bq, :heads]
        at_last = cum[:, bq - 1::bq, :heads]
        gap = at_first[:, :, None, :] - at_last[:, None, :, :]
        blocks = jnp.arange(nq, dtype=jnp.int32)
        earlier = (blocks[None, :] < blocks[:, None])[None, :, :, None]
        needed = jnp.logical_and(gap * LOG2E > -FOX_SKIP_BITS, earlier)
        first = jnp.where(jnp.any(needed, axis=2), jnp.argmax(needed, axis=2).astype(jnp.int32),
                          blocks[None, :, None])
        first = jnp.min(first.reshape(b, nq, pairs, 2), axis=-1)
        return jnp.transpose(first, (0, 2, 1)).reshape(-1)

    bound = FOX_BOUND_MARGIN * LOG2E * hd ** 0.5 * jnp.max(jnp.abs(qg)) * jnp.max(jnp.abs(kg))
    return lax.cond(bound <= FOX_BOUND_MAX,
                    lambda: call(True, first_key_blocks(), jnp.full((1, LANES), bound, F32)),
                    lambda: call(False))


def _gla_in_kernel(x_ref, ng_ref, sc_ref, sh_ref, w_ref, wg2_ref, bg_ref,
                   q_ref, k_ref, v_ref, g_ref, la_ref, *, dk, dv, tn):
    h = _norm_mod(x_ref, ng_ref, sc_ref, sh_ref)
    main = 2 * dk + 2 * dv
    r = _dot(h, w_ref[:, main:])
    _project(h, w_ref, 0, dk, q_ref, tn)
    z = _dot(r.astype(BF16), wg2_ref[...]) + bg_ref[...]
    _project(h, w_ref, dk, dk, k_ref, tn)
    la_ref[...] = _log_sigmoid(z) * (1.0 / GLA_TAU)
    _project(h, w_ref, 2 * dk, dv, v_ref, tn)
    _project(h, w_ref, 2 * dk + dv, dv, g_ref, tn)


def _gla_in_proj(x, norm_g, mod, layer, w_in, w_gate2, b_gate, tm=512, tn=512):
    b, s, d = x.shape
    dk, dv, rank = d // 2, d, GLA_RANK
    main = 2 * dk + 2 * dv
    w_pad = _cast_widen(w_in, LANES - rank)
    w_g2 = jnp.pad(w_gate2.astype(BF16), ((0, LANES - rank), (0, 0)))
    ng, modspec = _layer_vec_specs(layer, d)
    tok = lambda w, : pl.BlockSpec((None, tm, w), lambda bi, m: (bi, m, 0))
    return pl.pallas_call(
        functools.partial(_gla_in_kernel, dk=dk, dv=dv, tn=tn),
        grid=(b, s // tm),
        in_specs=[tok(d), ng, modspec(1), modspec(0), _resident((d, main + LANES)),
                  _resident((LANES, dk)), _resident((1, dk))],
        out_specs=[tok(dk), tok(dk), tok(dv), tok(dv), tok(dk)],
        out_shape=[jax.ShapeDtypeStruct((b, s, w), BF16) for w in (dk, dk, dv, dv)]
                  + [jax.ShapeDtypeStruct((b, s, dk), F32)],
        compiler_params=_params("parallel", "parallel"),
        name="gla_in_proj",
    )(x, norm_g, mod, mod, w_pad, w_g2, b_gate.astype(F32).reshape(1, dk))


def _gla_core_kernel(q_ref, k_ref, v_ref, g_ref, la_ref, x_ref, w_ref, gate_ref, fg_ref, o_ref,
                     state_ref, y_ref, *, heads, hk, hv, nchunk, final):
    @pl.when(pl.program_id(1) == 0)
    def _():
        state_ref[...] = jnp.zeros_like(state_ref)

    tile = nchunk * CHUNK
    row = lax.broadcasted_iota(jnp.int32, (tile, tile), 0)
    col = lax.broadcasted_iota(jnp.int32, (tile, tile), 1)
    same = (row // CHUNK) == (col // CHUNK)
    causal = row >= col
    tri = jnp.where(jnp.logical_and(same, causal), 1.0, 0.0).astype(BF16)
    scale = hk ** -0.5
    dk = heads * hk
    chunk_rows = [slice(c * CHUNK, (c + 1) * CHUNK) for c in range(nchunk)]

    def run_tile(rows):
        cb = _tri_cumsum(tri, la_ref[rows, :], pieces=2)
        lasts = [cb[r.stop - 1:r.stop, :] for r in chunk_rows]
        cb_last = jnp.concatenate([jnp.broadcast_to(last, (CHUNK, dk)) for last in lasts], axis=0)
        eb = jnp.exp(cb)
        enb = jnp.exp(-cb)
        qf = q_ref[rows, :].astype(F32) * scale
        kf = k_ref[rows, :].astype(F32)
        q_e = (qf * eb).astype(BF16)
        q_n = (qf * enb).astype(BF16)
        k_n = (kf * enb).astype(BF16)
        k_e = (kf * eb).astype(BF16)
        k_in = (kf * jnp.exp(cb_last - cb)).astype(BF16)
        for h in range(heads):
            ks = slice(h * hk, (h + 1) * hk)
            vs = slice(h * hv, (h + 1) * hv)
            v = v_ref[rows, vs]
            a_causal = lax.dot_general(q_e[:, ks], k_n[:, ks], NT_DIMS, preferred_element_type=F32)
            a_anti = lax.dot_general(q_n[:, ks], k_e[:, ks], NT_DIMS, preferred_element_type=F32)
            attn = jnp.where(same, jnp.where(causal, a_causal, a_anti), 0.0).astype(BF16)
            updates = [lax.dot_general(k_in[r, ks], v[r, :], TN_DIMS, preferred_element_type=F32)
                       for r in chunk_rows]
            o = _dot(attn, v)
            state = state_ref[h]
            carried = []
            for c in range(nchunk):
                carried.append(state.astype(BF16))
                decay_col = jnp.broadcast_to(jnp.exp(lasts[c][:, ks]), (hk, hk)).T
                state = state * jnp.concatenate([decay_col] * (hv // hk), axis=1) + updates[c]
            state_ref[h] = state
            o = o + jnp.concatenate([_dot(q_e[r, ks], carried[c]) for c, r in enumerate(chunk_rows)], axis=0)
            ms = jnp.mean(o * o, axis=-1, keepdims=True)
            gate = _silu(g_ref[rows, vs].astype(F32))
            y_ref[rows, vs] = (o * lax.rsqrt(ms + EPS) * gate).astype(BF16)

    for t0 in range(0, q_ref.shape[0], tile):
        run_tile(slice(t0, t0 + tile))
    _residual_out(y_ref, w_ref, x_ref, gate_ref, fg_ref, o_ref, final)


def _gla_core(q, k, v, g, la, x, w_out, mod, layer, final_g, final, tile=256, tl=512):
    b, s, dk = q.shape
    dv = v.shape[-1]
    d = x.shape[-1]
    heads = GLA_HEADS
    hk, hv = dk // heads, dv // heads
    _, modspec = _layer_vec_specs(layer, d)
    tok = lambda w: pl.BlockSpec((None, tl, w), lambda bi, l: (bi, l, 0))
    return pl.pallas_call(
        functools.partial(_gla_core_kernel, heads=heads, hk=hk, hv=hv, nchunk=tile // CHUNK, final=final),
        grid=(b, s // tl),
        in_specs=[tok(dk), tok(dk), tok(dv), tok(dv), tok(dk), tok(d), _resident((dv, d)), modspec(2),
                  _resident((1, d))],
        out_specs=tok(d),
        out_shape=jax.ShapeDtypeStruct((b, s, d), F32),
        scratch_shapes=[pltpu.VMEM((heads, hk, hv), F32), pltpu.VMEM((tl, dv), BF16)],
        compiler_params=_params("parallel", "arbitrary"),
        name="gla_core_out",
    )(q, k, v, g, la, x, w_out, mod, final_g)


def _out_kernel(y_ref, w_ref, x_ref, gate_ref, fg_ref, o_ref, *, final):
    _residual_out(y_ref, w_ref, x_ref, gate_ref, fg_ref, o_ref, final)


def _out_proj(y, w_out, x, mod, layer, final_g, final, tm=512):
    b, s, d = x.shape
    kdim = y.shape[-1]
    _, modspec = _layer_vec_specs(layer, d)
    tok = lambda w: pl.BlockSpec((None, tm, w), lambda bi, m: (bi, m, 0))
    return pl.pallas_call(
        functools.partial(_out_kernel, final=final),
        grid=(b, s // tm),
        in_specs=[tok(kdim), _resident((kdim, d)), tok(d), modspec(2), _resident((1, d))],
        out_specs=tok(d),
        out_shape=jax.ShapeDtypeStruct((b, s, d), F32),
        compiler_params=_params("parallel", "parallel"),
        name="out_proj_residual",
    )(y, w_out, x, mod, final_g)


def kernel(x, c, positions, mod_w, mod_b, norm_g, ret_w_in, ret_w_out, fox_w_in, fox_b_f, fox_q_gain,
           fox_k_gain, fox_w_out, gla_w_in, gla_w_gate2, gla_b_gate, gla_w_out, final_g):
    depth, d, _ = mod_w.shape
    b = x.shape[0]
    mod = _modulation(c, mod_w, mod_b).reshape(depth, b, 3, 1, d)
    ng = norm_g.astype(F32).reshape(depth, 1, d)
    fg = final_g.astype(F32).reshape(1, d)
    cos, sin = _rope_tables(positions, d // RET_HEADS // 2)
    ret_w_in_bf = ret_w_in.astype(BF16)
    ret_w_out_bf = ret_w_out.astype(BF16)
    for i in range(depth):
        j = i // N_MIXERS
        kind = i % N_MIXERS
        final = i == depth - 1
        if kind == 0:
            q, k, v, g = _ret_in_proj(x, cos, sin, ng, mod, i, ret_w_in_bf, j)
            x = _ret_core(q, k, v, g, x, ret_w_out_bf, j, mod, i, fg, final)
        elif kind == 1:
            q, k, v, g, cum = _fox_in_proj(x, ng, mod, i, fox_w_in[j], fox_b_f[j])
            y = _fox_flash(q, k, v, g, cum, fox_q_gain[j], fox_k_gain[j])
            x = _out_proj(y, fox_w_out[j].astype(BF16), x, mod, i, fg, final)
        else:
            q, k, v, g, la = _gla_in_proj(x, ng, mod, i, gla_w_in[j], gla_w_gate2[j], gla_b_gate[j])
            x = _gla_core(q, k, v, g, la, x, gla_w_out[j].astype(BF16), mod, i, fg, final)
    return x
```

```python
import functools
import math

import jax
import jax.numpy as jnp
from jax import lax
from jax.experimental import pallas as pl
from jax.experimental.pallas import tpu as pltpu

F32 = jnp.float32
BF16 = jnp.bfloat16

EPS = 1e-6
N_MIXERS = 3
CHUNK = 64
RET_HEADS = 4
ROPE_BASE = 10000.0
FOX_HEADS = 16
GLA_HEADS = 4
GLA_RANK = 16
GLA_TAU = 16.0

LANES = 128
RET_SUPER = 256
NEG_BIG = -1e30
LOG2E = math.log2(math.e)
FOX_BOUND_MARGIN = 1.02
FOX_BOUND_MAX = 48.0
FOX_SKIP_BITS = 160.0
VMEM_LIMIT = 56 * 1024 * 1024

NT_DIMS = (((1,), (1,)), ((), ()))
TN_DIMS = (((0,), (0,)), ((), ()))


def _silu(x):
    return x * (1.0 / (1.0 + jnp.exp(-x)))


def _log_sigmoid(z):
    return jnp.minimum(z, 0.0) - jnp.log1p(jnp.exp(-jnp.abs(z)))


def _dot(a, b):
    return jnp.dot(a, b, preferred_element_type=F32)


def _split3(x):
    hi = x.astype(BF16)
    r1 = x - hi.astype(F32)
    mid = r1.astype(BF16)
    lo = (r1 - mid.astype(F32)).astype(BF16)
    return hi, mid, lo


def _tri_cumsum(tri, x, pieces=3):
    parts = _split3(x)[:pieces]
    out = _dot(tri, parts[0])
    for part in parts[1:]:
        out = out + _dot(tri, part)
    return out


def _lower_tri(n):
    row = lax.broadcasted_iota(jnp.int32, (n, n), 0)
    col = lax.broadcasted_iota(jnp.int32, (n, n), 1)
    return row >= col


def _norm_mod(x_ref, ng_ref, sc_ref, sh_ref):
    x = x_ref[...]
    ms = jnp.mean(x * x, axis=-1, keepdims=True)
    y = x * lax.rsqrt(ms + EPS) * ng_ref[...]
    return (y * (1.0 + sc_ref[...]) + sh_ref[...]).astype(BF16)


def _project(h, w_ref, col0, width, o_ref, tn):
    for n0 in range(0, width, tn):
        o_ref[:, n0:n0 + tn] = _dot(h, w_ref[:, col0 + n0:col0 + n0 + tn]).astype(o_ref.dtype)


def _cast_once(w_ref, wbf_ref, tn=512):
    @pl.when(jnp.logical_and(pl.program_id(0) == 0, pl.program_id(1) == 0))
    def _():
        cols = w_ref.shape[1]
        aligned = cols // LANES * LANES
        for n0 in range(0, aligned, tn):
            n1 = min(n0 + tn, aligned)
            wbf_ref[:, n0:n1] = w_ref[:, n0:n1].astype(BF16)
        if aligned < cols:
            wbf_ref[:, aligned:] = jnp.zeros((w_ref.shape[0], wbf_ref.shape[1] - aligned), BF16)
            wbf_ref[:, aligned:cols] = w_ref[:, aligned:cols].astype(BF16)


def _params(*sem):
    return pltpu.CompilerParams(dimension_semantics=sem, vmem_limit_bytes=VMEM_LIMIT)


def _resident(shape, layer=None):
    nd = len(shape)
    if layer is None:
        return pl.BlockSpec(shape, lambda *_: (0,) * nd, pipeline_mode=pl.Buffered(1))
    return pl.BlockSpec((None,) + tuple(shape), lambda *_: (layer,) + (0,) * nd, pipeline_mode=pl.Buffered(1))


def _mod_kernel(c_ref, w_ref, b_ref, o_ref):
    a = _silu(c_ref[...])
    o_ref[...] = _dot(a.astype(BF16), w_ref[...].astype(BF16)) + b_ref[...]


def _modulation(c, mod_w, mod_b, tn=1024):
    depth, d, d3 = mod_w.shape
    b = c.shape[0]
    return pl.pallas_call(
        _mod_kernel,
        grid=(depth, d3 // tn),
        in_specs=[
            pl.BlockSpec((b, d), lambda i, n: (0, 0)),
            pl.BlockSpec((None, d, tn), lambda i, n: (i, 0, n)),
            pl.BlockSpec((None, 1, tn), lambda i, n: (i, 0, n)),
        ],
        out_specs=pl.BlockSpec((None, b, tn), lambda i, n: (i, 0, n)),
        out_shape=jax.ShapeDtypeStruct((depth, b, d3), F32),
        compiler_params=_params("parallel", "parallel"),
        name="adaln_modulation",
    )(c, mod_w, mod_b.reshape(depth, 1, d3))


def _layer_vec_specs(layer, d):
    ng = pl.BlockSpec((None, 1, d), lambda b, m: (layer, 0, 0))

    def mod(which):
        return pl.BlockSpec((None, None, None, 1, d), lambda b, m: (layer, b, which, 0, 0))
    return ng, mod


def _rope_kernel(pos_ref, inv_ref, cos_ref, sin_ref):
    ang = pos_ref[...].astype(F32) * inv_ref[...]
    cos_ref[...] = jnp.cos(ang)
    sin_ref[...] = jnp.sin(ang)


def _rope_tables(positions, half, tm=1024):
    b, s = positions.shape
    inv = (ROPE_BASE ** (-jnp.arange(half, dtype=F32) / half)).reshape(1, half)
    tok = lambda w: pl.BlockSpec((None, tm, w), lambda bi, m: (bi, m, 0))
    return pl.pallas_call(
        _rope_kernel,
        grid=(b, s // tm),
        in_specs=[tok(1), pl.BlockSpec((1, half), lambda bi, m: (0, 0))],
        out_specs=[tok(half), tok(half)],
        out_shape=[jax.ShapeDtypeStruct((b, s, half), F32)] * 2,
        compiler_params=_params("parallel", "parallel"),
        name="rope_tables",
    )(positions.reshape(b, s, 1), inv)


def _ret_in_kernel(x_ref, ng_ref, sc_ref, sh_ref, cos_ref, sin_ref, w_ref,
                   q_ref, k_ref, v_ref, g_ref, *, heads, hk, dv, tn):
    h = _norm_mod(x_ref, ng_ref, sc_ref, sh_ref)
    cos = cos_ref[...]
    sin = sin_ref[...]
    half = hk // 2
    dk = heads * hk
    _project(h, w_ref, 2 * dk, dv, v_ref, tn)
    for col0, o_ref, scale in ((0, q_ref, hk ** -0.5), (dk, k_ref, None)):
        for hd in range(heads):
            r = _dot(h, w_ref[:, col0 + hd * hk:col0 + (hd + 1) * hk])
            x1 = r[:, :half]
            x2 = r[:, half:]
            o1 = x1 * cos - x2 * sin
            o2 = x1 * sin + x2 * cos
            if scale is not None:
                o1 = o1 * scale
                o2 = o2 * scale
            o_ref[:, hd * hk:hd * hk + half] = o1.astype(BF16)
            o_ref[:, hd * hk + half:(hd + 1) * hk] = o2.astype(BF16)
    _project(h, w_ref, 2 * dk + dv, dv, g_ref, tn)


def _ret_in_proj(x, cos, sin, norm_g, mod, layer, w_in, j, tm=512, tn=512):
    b, s, d = x.shape
    heads = RET_HEADS
    dk = d
    dv = 2 * d
    hk = dk // heads
    n_in = w_in.shape[-1]
    half = hk // 2
    ng, modspec = _layer_vec_specs(layer, d)
    tok = lambda w: pl.BlockSpec((None, tm, w), lambda bi, m: (bi, m, 0))
    return pl.pallas_call(
        functools.partial(_ret_in_kernel, heads=heads, hk=hk, dv=dv, tn=tn),
        grid=(b, s // tm),
        in_specs=[tok(d), ng, modspec(1), modspec(0), tok(half), tok(half), _resident((d, n_in), j)],
        out_specs=[tok(dk), tok(dk), tok(dv), tok(dv)],
        out_shape=[jax.ShapeDtypeStruct((b, s, w), BF16) for w in (dk, dk, dv, dv)],
        compiler_params=_params("parallel", "parallel"),
        name="retention_in_proj",
    )(x, norm_g, mod, mod, cos, sin, w_in)


def _residual_out(y_ref, w_ref, x_ref, gate_ref, fg_ref, o_ref, final):
    kdim = y_ref.shape[-1]
    half = kdim // 2
    r = _dot(y_ref[:, :half], w_ref[:half, :]) + _dot(y_ref[:, half:], w_ref[half:, :])
    xn = x_ref[...] + gate_ref[...] * r
    if final:
        ms = jnp.mean(xn * xn, axis=-1, keepdims=True)
        xn = xn * lax.rsqrt(ms + EPS) * fg_ref[...]
    o_ref[...] = xn


def _ret_core_kernel(q_ref, k_ref, v_ref, g_ref, x_ref, wf32_ref, gate_ref, fg_ref, o_ref,
                     state_ref, dmask_ref, qdec_ref, kdec_ref, y_ref, w_ref, *, heads, hk, hv, sup, final):
    log_gamma = [math.log1p(-(2.0 ** (-5.0 - h))) for h in range(heads)]
    _cast_once(wf32_ref, w_ref)

    @pl.when(pl.program_id(1) == 0)
    def _():
        state_ref[...] = jnp.zeros_like(state_ref)
        t = lax.broadcasted_iota(jnp.int32, (sup, sup), 0)
        s = lax.broadcasted_iota(jnp.int32, (sup, sup), 1)
        dist = jnp.abs(t - s).astype(F32)
        visible = (s // CHUNK) <= (t // CHUNK)
        pos = lax.broadcasted_iota(jnp.int32, (sup, hk), 0).astype(F32)
        for h in range(heads):
            dmask_ref[h] = jnp.where(visible, jnp.exp(log_gamma[h] * dist), 0.0)
            qdec_ref[h] = jnp.exp(log_gamma[h] * (pos + 1.0))
            kdec_ref[h] = jnp.exp(log_gamma[h] * (sup - 1.0 - pos))

    for r0 in range(0, q_ref.shape[0], sup):
        rows = slice(r0, r0 + sup)
        for h in range(heads):
            q = q_ref[rows, h * hk:(h + 1) * hk]
            k = k_ref[rows, h * hk:(h + 1) * hk]
            v = v_ref[rows, h * hv:(h + 1) * hv]
            scores = lax.dot_general(q, k, NT_DIMS, preferred_element_type=F32) * dmask_ref[h]
            state = state_ref[h]
            q_in = (q.astype(F32) * qdec_ref[h]).astype(BF16)
            o = _dot(scores.astype(BF16), v) + _dot(q_in, state.astype(BF16))
            k_in = (k.astype(F32) * kdec_ref[h]).astype(BF16)
            state_ref[h] = math.exp(log_gamma[h] * sup) * state + lax.dot_general(
                k_in, v, TN_DIMS, preferred_element_type=F32)
            mu = jnp.mean(o, axis=-1, keepdims=True)
            oc = o - mu
            var = jnp.mean(oc * oc, axis=-1, keepdims=True)
            gate = _silu(g_ref[rows, h * hv:(h + 1) * hv].astype(F32))
            y_ref[rows, h * hv:(h + 1) * hv] = (oc * lax.rsqrt(var + EPS) * gate).astype(BF16)
    _residual_out(y_ref, w_ref, x_ref, gate_ref, fg_ref, o_ref, final)


def _ret_core(q, k, v, g, x, w_out, j, mod, layer, final_g, final, tl=2 * RET_SUPER):
    b, s, dk = q.shape
    dv = v.shape[-1]
    d = x.shape[-1]
    heads = RET_HEADS
    hk, hv = dk // heads, dv // heads
    sup = RET_SUPER
    _, modspec = _layer_vec_specs(layer, d)
    tok = lambda w: pl.BlockSpec((None, tl, w), lambda bi, l: (bi, l, 0))
    return pl.pallas_call(
        functools.partial(_ret_core_kernel, heads=heads, hk=hk, hv=hv, sup=sup, final=final),
        grid=(b, s // tl),
        in_specs=[tok(dk), tok(dk), tok(dv), tok(dv), tok(d), _resident((dv, d), j), modspec(2), _resident((1, d))],
        out_specs=tok(d),
        out_shape=jax.ShapeDtypeStruct((b, s, d), F32),
        scratch_shapes=[pltpu.VMEM((heads, hk, hv), F32), pltpu.VMEM((heads, sup, sup), F32),
                        pltpu.VMEM((heads, sup, hk), F32), pltpu.VMEM((heads, sup, hk), F32),
                        pltpu.VMEM((tl, dv), BF16), pltpu.VMEM((dv, d), BF16)],
        compiler_params=_params("arbitrary", "arbitrary"),
        name="retention_core_out",
    )(q, k, v, g, x, w_out, mod, final_g)


def _fox_in_kernel(x_ref, ng_ref, sc_ref, sh_ref, wf32_ref, bf_ref,
                   q_ref, k_ref, v_ref, g_ref, cum_ref, carry_ref, w_ref, *, width, tn):
    _cast_once(wf32_ref, w_ref)

    @pl.when(pl.program_id(1) == 0)
    def _():
        carry_ref[...] = jnp.zeros_like(carry_ref)

    h = _norm_mod(x_ref, ng_ref, sc_ref, sh_ref)
    log_f = _log_sigmoid(_dot(h, w_ref[:, 4 * width:]) + bf_ref[...])
    _project(h, w_ref, 0, width, q_ref, tn)
    rows = log_f.shape[0]
    tri = jnp.where(_lower_tri(rows), 1.0, 0.0).astype(BF16)
    cum = _tri_cumsum(tri, log_f) + carry_ref[...]
    carry_ref[...] = cum[rows - 1:rows, :]
    cum_ref[...] = cum
    for idx, o_ref in ((1, k_ref), (2, v_ref), (3, g_ref)):
        _project(h, w_ref, idx * width, width, o_ref, tn)


def _fox_in_proj(x, norm_g, mod, layer, w_in, b_f, tm=512, tn=512):
    b, s, d = x.shape
    heads = FOX_HEADS
    n_in = w_in.shape[1]
    width = (n_in - heads) // 4
    bias = jnp.pad(b_f.astype(F32), (0, LANES - heads)).reshape(1, LANES)
    ng, modspec = _layer_vec_specs(layer, d)
    tok = lambda w: pl.BlockSpec((None, tm, w), lambda bi, m: (bi, m, 0))
    return pl.pallas_call(
        functools.partial(_fox_in_kernel, width=width, tn=tn),
        grid=(b, s // tm),
        in_specs=[tok(d), ng, modspec(1), modspec(0), _resident((d, n_in)), _resident((1, LANES))],
        out_specs=[tok(width)] * 4 + [tok(LANES)],
        out_shape=[jax.ShapeDtypeStruct((b, s, width), BF16)] * 4 + [jax.ShapeDtypeStruct((b, s, LANES), F32)],
        scratch_shapes=[pltpu.VMEM((1, LANES), F32), pltpu.VMEM((d, 4 * width + LANES), BF16)],
        compiler_params=_params("arbitrary", "arbitrary"),
        name="fox_in_proj",
    )(x, norm_g, mod, mod, w_in, bias)


def _fox_flash_kernel(*refs, bq, hd, seq, bounded):
    if bounded:
        (start_ref, q_ref, k_ref, v_ref, g_ref, cum_ref, qg_ref, kg_ref, bound_ref, y_ref,
         kx_ref, vx_ref, qx_ref, acc_ref) = refs
    else:
        (q_ref, k_ref, v_ref, g_ref, cum_ref, qg_ref, kg_ref, y_ref,
         kx_ref, vx_ref, qx_ref, acc_ref, m_ref) = refs
    pair = pl.program_id(1)
    lane = lax.broadcasted_iota(jnp.int32, (1, LANES), 1)
    first = lane < hd
    sels = (first, jnp.logical_not(first))
    aux0 = (hd, 0)

    def head_norm(x, gain):
        sq = x * x
        s0 = jnp.sum(jnp.where(first, sq, 0.0), axis=-1, keepdims=True)
        s1 = jnp.sum(jnp.where(first, 0.0, sq), axis=-1, keepdims=True)
        ms = jnp.where(first, s0, s1) * (1.0 / hd)
        return x * lax.rsqrt(ms + EPS) * gain

    def bias_lanes(cum, offset):
        src = lax.broadcasted_iota(jnp.int32, (LANES, LANES), 0)
        dst = lax.broadcasted_iota(jnp.int32, (LANES, LANES), 1)
        blocks = []
        for t in range(3):
            hit = jnp.logical_or(jnp.logical_and(src == 2 * pair, dst == aux0[0] + offset + t),
                                 jnp.logical_and(src == 2 * pair + 1, dst == aux0[1] + offset + t))
            blocks.append(jnp.where(hit, 1.0, 0.0))
        select = jnp.concatenate(blocks, axis=0).astype(BF16)
        moved = _dot(jnp.concatenate(_split3(cum), axis=1), select)
        out = []
        for j in range(2):
            one0 = aux0[j] + 3 - offset
            ones = jnp.where(jnp.logical_and(lane >= one0, lane < one0 + 3), 1.0, 0.0)
            out.append(moved + ones)
        return out

    one = jnp.ones((), BF16)
    for r0 in range(0, seq, bq):
        kn = head_norm(k_ref[r0:r0 + bq, :].astype(F32), kg_ref[...])
        bias = bias_lanes(cum_ref[r0:r0 + bq, :] * (-LOG2E), 0)
        v = v_ref[r0:r0 + bq, :]
        for j in range(2):
            kx_ref[j, r0:r0 + bq, :] = jnp.where(sels[j], kn, bias[j]).astype(BF16)
            vx_ref[j, r0:r0 + bq, :] = jnp.where(sels[j], v, one)

    def query_block(qi, carry):
        q0 = pl.multiple_of(qi * bq, bq)
        qrows = pl.ds(q0, bq)
        qn = head_norm(q_ref[qrows, :].astype(F32), qg_ref[...]) * (hd ** -0.5 * LOG2E)
        cum_q = cum_ref[qrows, :] * LOG2E
        if bounded:
            cum_q = cum_q - bound_ref[...]
        bias = bias_lanes(cum_q, 3)
        for j in range(2):
            qx_ref[j] = jnp.where(sels[j], qn, bias[j]).astype(BF16)
        acc_ref[...] = jnp.zeros_like(acc_ref)
        if not bounded:
            m_ref[...] = jnp.full_like(m_ref, NEG_BIG)
        causal = _lower_tri(bq)

        def bounded_blocks(full, diagonal):
            half = bq // 2
            work = []
            for kb in full:
                r0 = pl.multiple_of(kb * bq, bq)
                work += [(j, slice(0, bq), pl.ds(r0, bq), None) for j in range(2)]
            if diagonal:
                for j in range(2):
                    work.append((j, slice(0, half), pl.ds(q0, half), causal[:half, :half]))
                    work.append((j, slice(half, bq), qrows, causal[half:, :]))
            logits = [lax.dot_general(qx_ref[j, rq, :], kx_ref[j, rk, :], NT_DIMS, preferred_element_type=F32)
                      for j, rq, rk, _ in work]
            add = {}
            for (j, rq, rk, mask), s in zip(work, logits):
                if mask is not None:
                    s = jnp.where(mask, s, NEG_BIG)
                pv = _dot(jnp.exp2(s).astype(BF16), vx_ref[j, rk, :])
                key = (j, rq.start, rq.stop)
                add[key] = pv if key not in add else add[key] + pv
            for (j, r0, r1), pv in add.items():
                acc_ref[j, r0:r1, :] += pv

        def step(kb, masked):
            rk = pl.ds(pl.multiple_of(kb * bq, bq), bq)
            logits = [lax.dot_general(qx_ref[j], kx_ref[j, rk, :], NT_DIMS, preferred_element_type=F32)
                      for j in range(2)]
            for j in range(2):
                s = jnp.where(causal, logits[j], NEG_BIG) if masked else logits[j]
                m_prev = m_ref[j]
                m_new = jnp.maximum(m_prev, jnp.max(s, axis=-1, keepdims=True))
                p = jnp.exp2(s - jnp.concatenate([m_new] * (bq // LANES), axis=1))
                acc_ref[j] = jnp.exp2(m_prev - m_new) * acc_ref[j] + _dot(p.astype(BF16), vx_ref[j, rk, :])
                m_ref[j] = m_new

        if bounded:
            start = start_ref[(pl.program_id(0) * pl.num_programs(1) + pair) * (seq // bq) + qi]
            count = qi - start

            def body(t, carry):
                bounded_blocks([start + 2 * t, start + 2 * t + 1], False)
                return carry

            lax.fori_loop(0, jnp.right_shift(count, 1), body, 0)
            odd = jnp.bitwise_and(count, 1) == 1

            @pl.when(odd)
            def _():
                bounded_blocks([qi - 1], True)

            @pl.when(jnp.logical_not(odd))
            def _():
                bounded_blocks([], True)
        else:
            def body(kb, carry):
                step(kb, False)
                return carry

            lax.fori_loop(0, qi, body, 0)
            step(qi, True)
        num = jnp.where(first, acc_ref[0], acc_ref[1])
        den = jnp.where(first, pltpu.roll(acc_ref[0], hd, axis=1), pltpu.roll(acc_ref[1], hd, axis=1))
        y_ref[qrows, :] = (num / den * _silu(g_ref[qrows, :].astype(F32))).astype(BF16)
        return carry

    lax.fori_loop(0, seq // bq, query_block, 0)


def _fox_flash(q, k, v, g, cum, q_gain, k_gain, bq=512):
    b, s, width = q.shape
    heads = FOX_HEADS
    hd = width // heads
    assert 2 * hd == LANES
    pairs = heads // 2
    qg = jnp.tile(q_gain.astype(F32), 2).reshape(1, LANES)
    kg = jnp.tile(k_gain.astype(F32), 2).reshape(1, LANES)
    nq = s // bq
    full = pl.BlockSpec((None, s, LANES), lambda bi, p, *_: (bi, 0, p))
    row = pl.BlockSpec((1, LANES), lambda bi, p, *_: (0, 0))

    def call(bounded, *extra):
        grid_spec = pltpu.PrefetchScalarGridSpec(
            num_scalar_prefetch=1 if bounded else 0,
            grid=(b, pairs),
            in_specs=[full, full, full, full, pl.BlockSpec((None, s, LANES), lambda bi, p, *_: (bi, 0, 0)),
                      row, row] + ([row] if bounded else []),
            out_specs=full,
            scratch_shapes=[pltpu.VMEM((2, s, LANES), BF16), pltpu.VMEM((2, s, LANES), BF16),
                            pltpu.VMEM((2, bq, LANES), BF16), pltpu.VMEM((2, bq, LANES), F32)]
                           + ([] if bounded else [pltpu.VMEM((2, bq, LANES), F32)]))
        return pl.pallas_call(
            functools.partial(_fox_flash_kernel, bq=bq, hd=hd, seq=s, bounded=bounded),
            grid_spec=grid_spec,
            out_shape=jax.ShapeDtypeStruct((b, s, width), BF16),
            compiler_params=_params("parallel", "parallel"),
            name="fox_flash_bounded" if bounded else "fox_flash_online",
        )(*extra[:1], q, k, v, g, cum, qg, kg, *extra[1:])

    def first_key_blocks():
        at_first = cum[:, ::bq, :heads]
        at_last = cum[:, bq - 1::bq, :heads]
        gap = at_first[:, :, None, :] - at_last[:, None, :, :]
        blocks = jnp.arange(nq, dtype=jnp.int32)
        earlier = (blocks[None, :] < blocks[:, None])[None, :, :, None]
        needed = jnp.logical_and(gap * LOG2E > -FOX_SKIP_BITS, earlier)
        first = jnp.where(jnp.any(needed, axis=2), jnp.argmax(needed, axis=2).astype(jnp.int32),
                          blocks[None, :, None])
        first = jnp.min(first.reshape(b, nq, pairs, 2), axis=-1)
        return jnp.transpose(first, (0, 2, 1)).reshape(-1)

    bound = FOX_BOUND_MARGIN * LOG2E * hd ** 0.5 * jnp.max(jnp.abs(qg)) * jnp.max(jnp.abs(kg))
    return lax.cond(bound <= FOX_BOUND_MAX,
                    lambda: call(True, first_key_blocks(), jnp.full((1, LANES), bound, F32)),
                    lambda: call(False))


def _gla_in_kernel(x_ref, ng_ref, sc_ref, sh_ref, wf32_ref, wg2f32_ref, bg_ref,
                   q_ref, k_ref, v_ref, g_ref, la_ref, w_ref, wg2_ref, *, dk, dv, tn):
    _cast_once(wf32_ref, w_ref)

    @pl.when(jnp.logical_and(pl.program_id(0) == 0, pl.program_id(1) == 0))
    def _():
        wg2_ref[...] = jnp.zeros_like(wg2_ref)
        wg2_ref[:wg2f32_ref.shape[0], :] = wg2f32_ref[...].astype(BF16)

    h = _norm_mod(x_ref, ng_ref, sc_ref, sh_ref)
    main = 2 * dk + 2 * dv
    r = _dot(h, w_ref[:, main:])
    _project(h, w_ref, 0, dk, q_ref, tn)
    z = _dot(r.astype(BF16), wg2_ref[...]) + bg_ref[...]
    _project(h, w_ref, dk, dk, k_ref, tn)
    la_ref[...] = _log_sigmoid(z) * (1.0 / GLA_TAU)
    _project(h, w_ref, 2 * dk, dv, v_ref, tn)
    _project(h, w_ref, 2 * dk + dv, dv, g_ref, tn)


def _gla_in_proj(x, norm_g, mod, layer, w_in, w_gate2, b_gate, tm=512, tn=512):
    b, s, d = x.shape
    dk, dv, rank = d // 2, d, GLA_RANK
    main = 2 * dk + 2 * dv
    ng, modspec = _layer_vec_specs(layer, d)
    tok = lambda w, : pl.BlockSpec((None, tm, w), lambda bi, m: (bi, m, 0))
    return pl.pallas_call(
        functools.partial(_gla_in_kernel, dk=dk, dv=dv, tn=tn),
        grid=(b, s // tm),
        in_specs=[tok(d), ng, modspec(1), modspec(0), _resident((d, main + rank)),
                  _resident((rank, dk)), _resident((1, dk))],
        out_specs=[tok(dk), tok(dk), tok(dv), tok(dv), tok(dk)],
        out_shape=[jax.ShapeDtypeStruct((b, s, w), BF16) for w in (dk, dk, dv, dv)]
                  + [jax.ShapeDtypeStruct((b, s, dk), F32)],
        scratch_shapes=[pltpu.VMEM((d, main + LANES), BF16), pltpu.VMEM((LANES, dk), BF16)],
        compiler_params=_params("arbitrary", "arbitrary"),
        name="gla_in_proj",
    )(x, norm_g, mod, mod, w_in, w_gate2, b_gate.astype(F32).reshape(1, dk))


def _gla_core_kernel(q_ref, k_ref, v_ref, g_ref, la_ref, x_ref, wf32_ref, gate_ref, fg_ref, o_ref,
                     state_ref, y_ref, w_ref, *, heads, hk, hv, nchunk, final):
    _cast_once(wf32_ref, w_ref)

    @pl.when(pl.program_id(1) == 0)
    def _():
        state_ref[...] = jnp.zeros_like(state_ref)

    tile = nchunk * CHUNK
    row = lax.broadcasted_iota(jnp.int32, (tile, tile), 0)
    col = lax.broadcasted_iota(jnp.int32, (tile, tile), 1)
    same = (row // CHUNK) == (col // CHUNK)
    causal = row >= col
    tri = jnp.where(jnp.logical_and(same, causal), 1.0, 0.0).astype(BF16)
    scale = hk ** -0.5
    dk = heads * hk
    chunk_rows = [slice(c * CHUNK, (c + 1) * CHUNK) for c in range(nchunk)]

    def run_tile(rows):
        cb = _tri_cumsum(tri, la_ref[rows, :], pieces=2)
        lasts = [cb[r.stop - 1:r.stop, :] for r in chunk_rows]
        cb_last = jnp.concatenate([jnp.broadcast_to(last, (CHUNK, dk)) for last in lasts], axis=0)
        eb = jnp.exp(cb)
        enb = jnp.exp(-cb)
        qf = q_ref[rows, :].astype(F32) * scale
        kf = k_ref[rows, :].astype(F32)
        q_e = (qf * eb).astype(BF16)
        q_n = (qf * enb).astype(BF16)
        k_n = (kf * enb).astype(BF16)
        k_e = (kf * eb).astype(BF16)
        k_in = (kf * jnp.exp(cb_last - cb)).astype(BF16)
        for h in range(heads):
            ks = slice(h * hk, (h + 1) * hk)
            vs = slice(h * hv, (h + 1) * hv)
            v = v_ref[rows, vs]
            a_causal = lax.dot_general(q_e[:, ks], k_n[:, ks], NT_DIMS, preferred_element_type=F32)
            a_anti = lax.dot_general(q_n[:, ks], k_e[:, ks], NT_DIMS, preferred_element_type=F32)
            attn = jnp.where(same, jnp.where(causal, a_causal, a_anti), 0.0).astype(BF16)
            updates = [lax.dot_general(k_in[r, ks], v[r, :], TN_DIMS, preferred_element_type=F32)
                       for r in chunk_rows]
            o = _dot(attn, v)
            state = state_ref[h]
            carried = []
            for c in range(nchunk):
                carried.append(state.astype(BF16))
                decay_col = jnp.broadcast_to(jnp.exp(lasts[c][:, ks]), (hk, hk)).T
                state = state * jnp.concatenate([decay_col] * (hv // hk), axis=1) + updates[c]
            state_ref[h] = state
            o = o + jnp.concatenate([_dot(q_e[r, ks], carried[c]) for c, r in enumerate(chunk_rows)], axis=0)
            ms = jnp.mean(o * o, axis=-1, keepdims=True)
            gate = _silu(g_ref[rows, vs].astype(F32))
            y_ref[rows, vs] = (o * lax.rsqrt(ms + EPS) * gate).astype(BF16)

    for t0 in range(0, q_ref.shape[0], tile):
        run_tile(slice(t0, t0 + tile))
    _residual_out(y_ref, w_ref, x_ref, gate_ref, fg_ref, o_ref, final)


def _gla_core(q, k, v, g, la, x, w_out, mod, layer, final_g, final, tile=256, tl=512):
    b, s, dk = q.shape
    dv = v.shape[-1]
    d = x.shape[-1]
    heads = GLA_HEADS
    hk, hv = dk // heads, dv // heads
    _, modspec = _layer_vec_specs(layer, d)
    tok = lambda w: pl.BlockSpec((None, tl, w), lambda bi, l: (bi, l, 0))
    return pl.pallas_call(
        functools.partial(_gla_core_kernel, heads=heads, hk=hk, hv=hv, nchunk=tile // CHUNK, final=final),
        grid=(b, s // tl),
        in_specs=[tok(dk), tok(dk), tok(dv), tok(dv), tok(dk), tok(d), _resident((dv, d)), modspec(2),
                  _resident((1, d))],
        out_specs=tok(d),
        out_shape=jax.ShapeDtypeStruct((b, s, d), F32),
        scratch_shapes=[pltpu.VMEM((heads, hk, hv), F32), pltpu.VMEM((tl, dv), BF16), pltpu.VMEM((dv, d), BF16)],
        compiler_params=_params("arbitrary", "arbitrary"),
        name="gla_core_out",
    )(q, k, v, g, la, x, w_out, mod, final_g)


def _out_kernel(y_ref, wf32_ref, x_ref, gate_ref, fg_ref, o_ref, w_ref, *, final):
    _cast_once(wf32_ref, w_ref)
    _residual_out(y_ref, w_ref, x_ref, gate_ref, fg_ref, o_ref, final)


def _out_proj(y, w_out, x, mod, layer, final_g, final, tm=512):
    b, s, d = x.shape
    kdim = y.shape[-1]
    _, modspec = _layer_vec_specs(layer, d)
    tok = lambda w: pl.BlockSpec((None, tm, w), lambda bi, m: (bi, m, 0))
    return pl.pallas_call(
        functools.partial(_out_kernel, final=final),
        grid=(b, s // tm),
        in_specs=[tok(kdim), _resident((kdim, d)), tok(d), modspec(2), _resident((1, d))],
        out_specs=tok(d),
        out_shape=jax.ShapeDtypeStruct((b, s, d), F32),
        scratch_shapes=[pltpu.VMEM((kdim, d), BF16)],
        compiler_params=_params("arbitrary", "arbitrary"),
        name="out_proj_residual",
    )(y, w_out, x, mod, final_g)


def kernel(x, c, positions, mod_w, mod_b, norm_g, ret_w_in, ret_w_out, fox_w_in, fox_b_f, fox_q_gain,
           fox_k_gain, fox_w_out, gla_w_in, gla_w_gate2, gla_b_gate, gla_w_out, final_g):
    depth, d, _ = mod_w.shape
    b = x.shape[0]
    mod = _modulation(c, mod_w, mod_b).reshape(depth, b, 3, 1, d)
    ng = norm_g.astype(F32).reshape(depth, 1, d)
    fg = final_g.astype(F32).reshape(1, d)
    cos, sin = _rope_tables(positions, d // RET_HEADS // 2)
    ret_w_in_bf = ret_w_in.astype(BF16)
    for i in range(depth):
        j = i // N_MIXERS
        kind = i % N_MIXERS
        final = i == depth - 1
        if kind == 0:
            q, k, v, g = _ret_in_proj(x, cos, sin, ng, mod, i, ret_w_in_bf, j)
            x = _ret_core(q, k, v, g, x, ret_w_out, j, mod, i, fg, final)
        elif kind == 1:
            q, k, v, g, cum = _fox_in_proj(x, ng, mod, i, fox_w_in[j], fox_b_f[j])
            y = _fox_flash(q, k, v, g, cum, fox_q_gain[j], fox_k_gain[j])
            x = _out_proj(y, fox_w_out[j], x, mod, i, fg, final)
        else:
            q, k, v, g, la = _gla_in_proj(x, ng, mod, i, gla_w_in[j], gla_w_gate2[j], gla_b_gate[j])
            x = _gla_core(q, k, v, g, la, x, gla_w_out[j], mod, i, fg, final)
    return x
```

```python
import functools
import math

import jax
import jax.numpy as jnp
from jax import lax
from jax.experimental import pallas as pl
from jax.experimental.pallas import tpu as pltpu

F32 = jnp.float32
BF16 = jnp.bfloat16

EPS = 1e-6
N_MIXERS = 3
CHUNK = 64
RET_HEADS = 4
ROPE_BASE = 10000.0
FOX_HEADS = 16
GLA_HEADS = 4
GLA_RANK = 16
GLA_TAU = 16.0

LANES = 128
RET_SUPER = 256
NEG_BIG = -1e30
LOG2E = math.log2(math.e)
FOX_BOUND_MARGIN = 1.02
FOX_BOUND_MAX = 48.0
FOX_SKIP_BITS = 160.0
VMEM_LIMIT = 56 * 1024 * 1024

NT_DIMS = (((1,), (1,)), ((), ()))
TN_DIMS = (((0,), (0,)), ((), ()))


def _silu(x):
    return x * (1.0 / (1.0 + jnp.exp(-x)))


def _log_sigmoid(z):
    return jnp.minimum(z, 0.0) - jnp.log1p(jnp.exp(-jnp.abs(z)))


def _dot(a, b):
    return jnp.dot(a, b, preferred_element_type=F32)


def _split3(x):
    hi = x.astype(BF16)
    r1 = x - hi.astype(F32)
    mid = r1.astype(BF16)
    lo = (r1 - mid.astype(F32)).astype(BF16)
    return hi, mid, lo


def _tri_cumsum(tri, x, pieces=3):
    parts = _split3(x)[:pieces]
    out = _dot(tri, parts[0])
    for part in parts[1:]:
        out = out + _dot(tri, part)
    return out


def _lower_tri(n):
    row = lax.broadcasted_iota(jnp.int32, (n, n), 0)
    col = lax.broadcasted_iota(jnp.int32, (n, n), 1)
    return row >= col


def _norm_mod(x_ref, ng_ref, sc_ref, sh_ref):
    x = x_ref[...]
    ms = jnp.mean(x * x, axis=-1, keepdims=True)
    y = x * lax.rsqrt(ms + EPS) * ng_ref[...]
    return (y * (1.0 + sc_ref[...]) + sh_ref[...]).astype(BF16)


def _project(h, w_ref, col0, width, o_ref, tn):
    for n0 in range(0, width, tn):
        o_ref[:, n0:n0 + tn] = _dot(h, w_ref[:, col0 + n0:col0 + n0 + tn]).astype(o_ref.dtype)


def _cast_once(w_ref, wbf_ref, tn=512):
    @pl.when(jnp.logical_and(pl.program_id(0) == 0, pl.program_id(1) == 0))
    def _():
        cols = w_ref.shape[1]
        aligned = cols // LANES * LANES
        for n0 in range(0, aligned, tn):
            n1 = min(n0 + tn, aligned)
            wbf_ref[:, n0:n1] = w_ref[:, n0:n1].astype(BF16)
        if aligned < cols:
            wbf_ref[:, aligned:] = jnp.zeros((w_ref.shape[0], wbf_ref.shape[1] - aligned), BF16)
            wbf_ref[:, aligned:cols] = w_ref[:, aligned:cols].astype(BF16)


def _params(*sem):
    return pltpu.CompilerParams(dimension_semantics=sem, vmem_limit_bytes=VMEM_LIMIT)


def _resident(shape, layer=None):
    nd = len(shape)
    if layer is None:
        return pl.BlockSpec(shape, lambda *_: (0,) * nd, pipeline_mode=pl.Buffered(1))
    return pl.BlockSpec((None,) + tuple(shape), lambda *_: (layer,) + (0,) * nd, pipeline_mode=pl.Buffered(1))


def _mod_kernel(c_ref, w_ref, b_ref, o_ref):
    a = _silu(c_ref[...])
    o_ref[...] = _dot(a.astype(BF16), w_ref[...].astype(BF16)) + b_ref[...]


def _modulation(c, mod_w, mod_b, tn=1024):
    depth, d, d3 = mod_w.shape
    b = c.shape[0]
    return pl.pallas_call(
        _mod_kernel,
        grid=(depth, d3 // tn),
        in_specs=[
            pl.BlockSpec((b, d), lambda i, n: (0, 0)),
            pl.BlockSpec((None, d, tn), lambda i, n: (i, 0, n)),
            pl.BlockSpec((None, 1, tn), lambda i, n: (i, 0, n)),
        ],
        out_specs=pl.BlockSpec((None, b, tn), lambda i, n: (i, 0, n)),
        out_shape=jax.ShapeDtypeStruct((depth, b, d3), F32),
        compiler_params=_params("parallel", "parallel"),
        name="adaln_modulation",
    )(c, mod_w, mod_b.reshape(depth, 1, d3))


def _layer_vec_specs(layer, d):
    ng = pl.BlockSpec((None, 1, d), lambda b, m: (layer, 0, 0))

    def mod(which):
        return pl.BlockSpec((None, None, None, 1, d), lambda b, m: (layer, b, which, 0, 0))
    return ng, mod


def _rope_kernel(pos_ref, inv_ref, cos_ref, sin_ref):
    ang = pos_ref[...].astype(F32) * inv_ref[...]
    cos_ref[...] = jnp.cos(ang)
    sin_ref[...] = jnp.sin(ang)


def _rope_tables(positions, half, tm=1024):
    b, s = positions.shape
    inv = (ROPE_BASE ** (-jnp.arange(half, dtype=F32) / half)).reshape(1, half)
    tok = lambda w: pl.BlockSpec((None, tm, w), lambda bi, m: (bi, m, 0))
    return pl.pallas_call(
        _rope_kernel,
        grid=(b, s // tm),
        in_specs=[tok(1), pl.BlockSpec((1, half), lambda bi, m: (0, 0))],
        out_specs=[tok(half), tok(half)],
        out_shape=[jax.ShapeDtypeStruct((b, s, half), F32)] * 2,
        compiler_params=_params("parallel", "parallel"),
        name="rope_tables",
    )(positions.reshape(b, s, 1), inv)


def _ret_in_kernel(x_ref, ng_ref, sc_ref, sh_ref, cos_ref, sin_ref, w_ref,
                   q_ref, k_ref, v_ref, g_ref, *, heads, hk, dv, tn):
    h = _norm_mod(x_ref, ng_ref, sc_ref, sh_ref)
    cos = cos_ref[...]
    sin = sin_ref[...]
    half = hk // 2
    dk = heads * hk
    _project(h, w_ref, 2 * dk, dv, v_ref, tn)
    for col0, o_ref, scale in ((0, q_ref, hk ** -0.5), (dk, k_ref, None)):
        for hd in range(heads):
            r = _dot(h, w_ref[:, col0 + hd * hk:col0 + (hd + 1) * hk])
            x1 = r[:, :half]
            x2 = r[:, half:]
            o1 = x1 * cos - x2 * sin
            o2 = x1 * sin + x2 * cos
            if scale is not None:
                o1 = o1 * scale
                o2 = o2 * scale
            o_ref[:, hd * hk:hd * hk + half] = o1.astype(BF16)
            o_ref[:, hd * hk + half:(hd + 1) * hk] = o2.astype(BF16)
    _project(h, w_ref, 2 * dk + dv, dv, g_ref, tn)


def _ret_in_proj(x, cos, sin, norm_g, mod, layer, w_in, j, tm=512, tn=512):
    b, s, d = x.shape
    heads = RET_HEADS
    dk = d
    dv = 2 * d
    hk = dk // heads
    n_in = w_in.shape[-1]
    half = hk // 2
    ng, modspec = _layer_vec_specs(layer, d)
    tok = lambda w: pl.BlockSpec((None, tm, w), lambda bi, m: (bi, m, 0))
    return pl.pallas_call(
        functools.partial(_ret_in_kernel, heads=heads, hk=hk, dv=dv, tn=tn),
        grid=(b, s // tm),
        in_specs=[tok(d), ng, modspec(1), modspec(0), tok(half), tok(half), _resident((d, n_in), j)],
        out_specs=[tok(dk), tok(dk), tok(dv), tok(dv)],
        out_shape=[jax.ShapeDtypeStruct((b, s, w), BF16) for w in (dk, dk, dv, dv)],
        compiler_params=_params("parallel", "parallel"),
        name="retention_in_proj",
    )(x, norm_g, mod, mod, cos, sin, w_in)


def _residual_out(y_ref, w_ref, x_ref, gate_ref, fg_ref, o_ref, final):
    kdim = y_ref.shape[-1]
    half = kdim // 2
    r = _dot(y_ref[:, :half], w_ref[:half, :]) + _dot(y_ref[:, half:], w_ref[half:, :])
    xn = x_ref[...] + gate_ref[...] * r
    if final:
        ms = jnp.mean(xn * xn, axis=-1, keepdims=True)
        xn = xn * lax.rsqrt(ms + EPS) * fg_ref[...]
    o_ref[...] = xn


def _ret_core_kernel(q_ref, k_ref, v_ref, g_ref, x_ref, wf32_ref, gate_ref, fg_ref, o_ref,
                     state_ref, dmask_ref, qdec_ref, kdec_ref, y_ref, w_ref, *, heads, hk, hv, sup, final):
    log_gamma = [math.log1p(-(2.0 ** (-5.0 - h))) for h in range(heads)]
    _cast_once(wf32_ref, w_ref)

    @pl.when(pl.program_id(1) == 0)
    def _():
        state_ref[...] = jnp.zeros_like(state_ref)
        t = lax.broadcasted_iota(jnp.int32, (sup, sup), 0)
        s = lax.broadcasted_iota(jnp.int32, (sup, sup), 1)
        dist = jnp.abs(t - s).astype(F32)
        visible = (s // CHUNK) <= (t // CHUNK)
        pos = lax.broadcasted_iota(jnp.int32, (sup, hk), 0).astype(F32)
        for h in range(heads):
            dmask_ref[h] = jnp.where(visible, jnp.exp(log_gamma[h] * dist), 0.0)
            qdec_ref[h] = jnp.exp(log_gamma[h] * (pos + 1.0))
            kdec_ref[h] = jnp.exp(log_gamma[h] * (sup - 1.0 - pos))

    for r0 in range(0, q_ref.shape[0], sup):
        rows = slice(r0, r0 + sup)
        for h in range(heads):
            q = q_ref[rows, h * hk:(h + 1) * hk]
            k = k_ref[rows, h * hk:(h + 1) * hk]
            v = v_ref[rows, h * hv:(h + 1) * hv]
            scores = lax.dot_general(q, k, NT_DIMS, preferred_element_type=F32) * dmask_ref[h]
            state = state_ref[h]
            q_in = (q.astype(F32) * qdec_ref[h]).astype(BF16)
            o = _dot(scores.astype(BF16), v) + _dot(q_in, state.astype(BF16))
            k_in = (k.astype(F32) * kdec_ref[h]).astype(BF16)
            state_ref[h] = math.exp(log_gamma[h] * sup) * state + lax.dot_general(
                k_in, v, TN_DIMS, preferred_element_type=F32)
            mu = jnp.mean(o, axis=-1, keepdims=True)
            oc = o - mu
            var = jnp.mean(oc * oc, axis=-1, keepdims=True)
            gate = _silu(g_ref[rows, h * hv:(h + 1) * hv].astype(F32))
            y_ref[rows, h * hv:(h + 1) * hv] = (oc * lax.rsqrt(var + EPS) * gate).astype(BF16)
    _residual_out(y_ref, w_ref, x_ref, gate_ref, fg_ref, o_ref, final)


def _ret_core(q, k, v, g, x, w_out, j, mod, layer, final_g, final, tl=2 * RET_SUPER):
    b, s, dk = q.shape
    dv = v.shape[-1]
    d = x.shape[-1]
    heads = RET_HEADS
    hk, hv = dk // heads, dv // heads
    sup = RET_SUPER
    _, modspec = _layer_vec_specs(layer, d)
    tok = lambda w: pl.BlockSpec((None, tl, w), lambda bi, l: (bi, l, 0))
    return pl.pallas_call(
        functools.partial(_ret_core_kernel, heads=heads, hk=hk, hv=hv, sup=sup, final=final),
        grid=(b, s // tl),
        in_specs=[tok(dk), tok(dk), tok(dv), tok(dv), tok(d), _resident((dv, d), j), modspec(2), _resident((1, d))],
        out_specs=tok(d),
        out_shape=jax.ShapeDtypeStruct((b, s, d), F32),
        scratch_shapes=[pltpu.VMEM((heads, hk, hv), F32), pltpu.VMEM((heads, sup, sup), F32),
                        pltpu.VMEM((heads, sup, hk), F32), pltpu.VMEM((heads, sup, hk), F32),
                        pltpu.VMEM((tl, dv), BF16), pltpu.VMEM((dv, d), BF16)],
        compiler_params=_params("arbitrary", "arbitrary"),
        name="retention_core_out",
    )(q, k, v, g, x, w_out, mod, final_g)


def _fox_in_kernel(x_ref, ng_ref, sc_ref, sh_ref, wf32_ref, bf_ref,
                   q_ref, k_ref, v_ref, g_ref, cum_ref, first_ref, last_ref, carry_ref, w_ref, *, width, tn):
    _cast_once(wf32_ref, w_ref)

    @pl.when(pl.program_id(1) == 0)
    def _():
        carry_ref[...] = jnp.zeros_like(carry_ref)

    h = _norm_mod(x_ref, ng_ref, sc_ref, sh_ref)
    log_f = _log_sigmoid(_dot(h, w_ref[:, 4 * width:]) + bf_ref[...])
    _project(h, w_ref, 0, width, q_ref, tn)
    rows = log_f.shape[0]
    tri = jnp.where(_lower_tri(rows), 1.0, 0.0).astype(BF16)
    cum = _tri_cumsum(tri, log_f) + carry_ref[...]
    carry_ref[...] = cum[rows - 1:rows, :]
    cum_ref[...] = cum
    first_ref[...] = cum[0:1, :]
    last_ref[0:1, :] = cum[rows // 2 - 1:rows // 2, :]
    last_ref[1:2, :] = cum[rows - 1:rows, :]
    for idx, o_ref in ((1, k_ref), (2, v_ref), (3, g_ref)):
        _project(h, w_ref, idx * width, width, o_ref, tn)


def _fox_in_proj(x, norm_g, mod, layer, w_in, j, b_f, tm=512, tn=512):
    b, s, d = x.shape
    heads = FOX_HEADS
    n_in = w_in.shape[-1]
    width = (n_in - heads) // 4
    bias = jnp.pad(b_f.astype(F32), (0, LANES - heads)).reshape(1, LANES)
    ng, modspec = _layer_vec_specs(layer, d)
    tok = lambda w: pl.BlockSpec((None, tm, w), lambda bi, m: (bi, m, 0))
    edge = lambda n: pl.BlockSpec((None, None, n, LANES), lambda bi, m: (bi, m, 0, 0))
    return pl.pallas_call(
        functools.partial(_fox_in_kernel, width=width, tn=tn),
        grid=(b, s // tm),
        in_specs=[tok(d), ng, modspec(1), modspec(0), _resident((d, n_in), j), _resident((1, LANES))],
        out_specs=[tok(width)] * 4 + [tok(LANES), edge(1), edge(2)],
        out_shape=[jax.ShapeDtypeStruct((b, s, width), BF16)] * 4 + [jax.ShapeDtypeStruct((b, s, LANES), F32)]
                  + [jax.ShapeDtypeStruct((b, s // tm, n, LANES), F32) for n in (1, 2)],
        scratch_shapes=[pltpu.VMEM((1, LANES), F32), pltpu.VMEM((d, 4 * width + LANES), BF16)],
        compiler_params=_params("arbitrary", "arbitrary"),
        name="fox_in_proj",
    )(x, norm_g, mod, mod, w_in, bias)


def _fox_flash_kernel(*refs, bq, hd, seq, bounded):
    if bounded:
        (start_ref, q_ref, k_ref, v_ref, g_ref, cum_ref, qg_ref, kg_ref, bound_ref, y_ref,
         kx_ref, vx_ref, qx_ref, acc_ref) = refs
    else:
        (q_ref, k_ref, v_ref, g_ref, cum_ref, qg_ref, kg_ref, y_ref,
         kx_ref, vx_ref, qx_ref, acc_ref, m_ref) = refs
    pair = pl.program_id(1)
    lane = lax.broadcasted_iota(jnp.int32, (1, LANES), 1)
    first = lane < hd
    sels = (first, jnp.logical_not(first))
    aux0 = (hd, 0)

    def head_norm(x, gain):
        sq = x * x
        s0 = jnp.sum(jnp.where(first, sq, 0.0), axis=-1, keepdims=True)
        s1 = jnp.sum(jnp.where(first, 0.0, sq), axis=-1, keepdims=True)
        ms = jnp.where(first, s0, s1) * (1.0 / hd)
        return x * lax.rsqrt(ms + EPS) * gain

    def bias_lanes(cum, offset):
        src = lax.broadcasted_iota(jnp.int32, (LANES, LANES), 0)
        dst = lax.broadcasted_iota(jnp.int32, (LANES, LANES), 1)
        blocks = []
        for t in range(3):
            hit = jnp.logical_or(jnp.logical_and(src == 2 * pair, dst == aux0[0] + offset + t),
                                 jnp.logical_and(src == 2 * pair + 1, dst == aux0[1] + offset + t))
            blocks.append(jnp.where(hit, 1.0, 0.0))
        select = jnp.concatenate(blocks, axis=0).astype(BF16)
        moved = _dot(jnp.concatenate(_split3(cum), axis=1), select)
        out = []
        for j in range(2):
            one0 = aux0[j] + 3 - offset
            ones = jnp.where(jnp.logical_and(lane >= one0, lane < one0 + 3), 1.0, 0.0)
            out.append(moved + ones)
        return out

    one = jnp.ones((), BF16)
    for r0 in range(0, seq, bq):
        kn = head_norm(k_ref[r0:r0 + bq, :].astype(F32), kg_ref[...])
        bias = bias_lanes(cum_ref[r0:r0 + bq, :] * (-LOG2E), 0)
        v = v_ref[r0:r0 + bq, :]
        for j in range(2):
            kx_ref[j, r0:r0 + bq, :] = jnp.where(sels[j], kn, bias[j]).astype(BF16)
            vx_ref[j, r0:r0 + bq, :] = jnp.where(sels[j], v, one)

    def query_block(qi, carry):
        q0 = pl.multiple_of(qi * bq, bq)
        qrows = pl.ds(q0, bq)
        qn = head_norm(q_ref[qrows, :].astype(F32), qg_ref[...]) * (hd ** -0.5 * LOG2E)
        cum_q = cum_ref[qrows, :] * LOG2E
        if bounded:
            cum_q = cum_q - bound_ref[...]
        bias = bias_lanes(cum_q, 3)
        for j in range(2):
            qx_ref[j] = jnp.where(sels[j], qn, bias[j]).astype(BF16)
        acc_ref[...] = jnp.zeros_like(acc_ref)
        if not bounded:
            m_ref[...] = jnp.full_like(m_ref, NEG_BIG)
        causal = _lower_tri(bq)

        def bounded_blocks(full, diagonal, lead_half=None):
            half = bq // 2
            work = []
            if lead_half is not None:
                rk = pl.ds(pl.multiple_of(lead_half * half, half), half)
                work += [(j, slice(0, bq), rk, None) for j in range(2)]
            for kb in full:
                r0 = pl.multiple_of(kb * bq, bq)
                work += [(j, slice(0, bq), pl.ds(r0, bq), None) for j in range(2)]
            if diagonal:
                for j in range(2):
                    work.append((j, slice(0, half), pl.ds(q0, half), causal[:half, :half]))
                    work.append((j, slice(half, bq), qrows, causal[half:, :]))
            logits = [lax.dot_general(qx_ref[j, rq, :], kx_ref[j, rk, :], NT_DIMS, preferred_element_type=F32)
                      for j, rq, rk, _ in work]
            add = {}
            for (j, rq, rk, mask), s in zip(work, logits):
                if mask is not None:
                    s = jnp.where(mask, s, NEG_BIG)
                pv = _dot(jnp.exp2(s).astype(BF16), vx_ref[j, rk, :])
                key = (j, rq.start, rq.stop)
                add[key] = pv if key not in add else add[key] + pv
            for (j, r0, r1), pv in add.items():
                acc_ref[j, r0:r1, :] += pv

        def step(kb, masked):
            rk = pl.ds(pl.multiple_of(kb * bq, bq), bq)
            logits = [lax.dot_general(qx_ref[j], kx_ref[j, rk, :], NT_DIMS, preferred_element_type=F32)
                      for j in range(2)]
            for j in range(2):
                s = jnp.where(causal, logits[j], NEG_BIG) if masked else logits[j]
                m_prev = m_ref[j]
                m_new = jnp.maximum(m_prev, jnp.max(s, axis=-1, keepdims=True))
                p = jnp.exp2(s - jnp.concatenate([m_new] * (bq // LANES), axis=1))
                acc_ref[j] = jnp.exp2(m_prev - m_new) * acc_ref[j] + _dot(p.astype(BF16), vx_ref[j, rk, :])
                m_ref[j] = m_new

        if bounded:
            first_half = start_ref[(pl.program_id(0) * pl.num_programs(1) + pair) * (seq // bq) + qi]
            lead = jnp.bitwise_and(first_half, 1) == 1
            start = jnp.right_shift(first_half + 1, 1)
            count = qi - start

            def body(t, carry):
                bounded_blocks([start + 2 * t, start + 2 * t + 1], False)
                return carry

            lax.fori_loop(0, jnp.right_shift(count, 1), body, 0)
            odd = jnp.bitwise_and(count, 1) == 1
            for is_odd in (True, False):
                for has_lead in (True, False):
                    @pl.when(jnp.logical_and(odd == is_odd, lead == has_lead))
                    def _():
                        bounded_blocks([qi - 1] if is_odd else [], True, first_half if has_lead else None)
        else:
            def body(kb, carry):
                step(kb, False)
                return carry

            lax.fori_loop(0, qi, body, 0)
            step(qi, True)
        num = jnp.where(first, acc_ref[0], acc_ref[1])
        den = jnp.where(first, pltpu.roll(acc_ref[0], hd, axis=1), pltpu.roll(acc_ref[1], hd, axis=1))
        y_ref[qrows, :] = (num / den * _silu(g_ref[qrows, :].astype(F32))).astype(BF16)
        return carry

    lax.fori_loop(0, seq // bq, query_block, 0)


def _fox_flash(q, k, v, g, cum, cum_first, cum_last, q_gain, k_gain, bq=512):
    b, s, width = q.shape
    assert cum_first.shape[1] == s // bq and cum_last.shape[2] == 2
    heads = FOX_HEADS
    hd = width // heads
    assert 2 * hd == LANES
    pairs = heads // 2
    qg = jnp.tile(q_gain.astype(F32), 2).reshape(1, LANES)
    kg = jnp.tile(k_gain.astype(F32), 2).reshape(1, LANES)
    nq = s // bq
    full = pl.BlockSpec((None, s, LANES), lambda bi, p, *_: (bi, 0, p))
    row = pl.BlockSpec((1, LANES), lambda bi, p, *_: (0, 0))

    def call(bounded, *extra):
        grid_spec = pltpu.PrefetchScalarGridSpec(
            num_scalar_prefetch=1 if bounded else 0,
            grid=(b, pairs),
            in_specs=[full, full, full, full, pl.BlockSpec((None, s, LANES), lambda bi, p, *_: (bi, 0, 0)),
                      row, row] + ([row] if bounded else []),
            out_specs=full,
            scratch_shapes=[pltpu.VMEM((2, s, LANES), BF16), pltpu.VMEM((2, s, LANES), BF16),
                            pltpu.VMEM((2, bq, LANES), BF16), pltpu.VMEM((2, bq, LANES), F32)]
                           + ([] if bounded else [pltpu.VMEM((2, bq, LANES), F32)]))
        return pl.pallas_call(
            functools.partial(_fox_flash_kernel, bq=bq, hd=hd, seq=s, bounded=bounded),
            grid_spec=grid_spec,
            out_shape=jax.ShapeDtypeStruct((b, s, width), BF16),
            compiler_params=_params("parallel", "parallel"),
            name="fox_flash_bounded" if bounded else "fox_flash_online",
        )(*extra[:1], q, k, v, g, cum, qg, kg, *extra[1:])

    def first_key_blocks():
        at_first = cum_first[:, :, 0, :heads]
        at_last = cum_last[:, :, :, :heads].reshape(b, 2 * nq, heads)
        gap = at_first[:, :, None, :] - at_last[:, None, :, :]
        blocks = jnp.arange(nq, dtype=jnp.int32)
        halves = jnp.arange(2 * nq, dtype=jnp.int32)
        earlier = (halves[None, :] < 2 * blocks[:, None])[None, :, :, None]
        needed = jnp.logical_and(gap * LOG2E > -FOX_SKIP_BITS, earlier)
        first = jnp.where(jnp.any(needed, axis=2), jnp.argmax(needed, axis=2).astype(jnp.int32),
                          2 * blocks[None, :, None])
        first = jnp.min(first.reshape(b, nq, pairs, 2), axis=-1)
        return jnp.transpose(first, (0, 2, 1)).reshape(-1)

    bound = FOX_BOUND_MARGIN * LOG2E * hd ** 0.5 * jnp.max(jnp.abs(qg)) * jnp.max(jnp.abs(kg))
    return lax.cond(bound <= FOX_BOUND_MAX,
                    lambda: call(True, first_key_blocks(), jnp.full((1, LANES), bound, F32)),
                    lambda: call(False))


def _gla_in_kernel(x_ref, ng_ref, sc_ref, sh_ref, wf32_ref, wg2f32_ref, bg_ref,
                   q_ref, k_ref, v_ref, g_ref, la_ref, w_ref, wg2_ref, *, dk, dv, tn):
    _cast_once(wf32_ref, w_ref)

    @pl.when(jnp.logical_and(pl.program_id(0) == 0, pl.program_id(1) == 0))
    def _():
        wg2_ref[...] = jnp.zeros_like(wg2_ref)
        wg2_ref[:wg2f32_ref.shape[0], :] = wg2f32_ref[...].astype(BF16)

    h = _norm_mod(x_ref, ng_ref, sc_ref, sh_ref)
    main = 2 * dk + 2 * dv
    r = _dot(h, w_ref[:, main:])
    _project(h, w_ref, 0, dk, q_ref, tn)
    z = _dot(r.astype(BF16), wg2_ref[...]) + bg_ref[...]
    _project(h, w_ref, dk, dk, k_ref, tn)
    la_ref[...] = _log_sigmoid(z) * (1.0 / GLA_TAU)
    _project(h, w_ref, 2 * dk, dv, v_ref, tn)
    _project(h, w_ref, 2 * dk + dv, dv, g_ref, tn)


def _gla_in_proj(x, norm_g, mod, layer, w_in, w_gate2, j, b_gate, tm=512, tn=512):
    b, s, d = x.shape
    dk, dv, rank = d // 2, d, GLA_RANK
    main = 2 * dk + 2 * dv
    ng, modspec = _layer_vec_specs(layer, d)
    tok = lambda w, : pl.BlockSpec((None, tm, w), lambda bi, m: (bi, m, 0))
    return pl.pallas_call(
        functools.partial(_gla_in_kernel, dk=dk, dv=dv, tn=tn),
        grid=(b, s // tm),
        in_specs=[tok(d), ng, modspec(1), modspec(0), _resident((d, main + rank), j),
                  _resident((rank, dk), j), _resident((1, dk))],
        out_specs=[tok(dk), tok(dk), tok(dv), tok(dv), tok(dk)],
        out_shape=[jax.ShapeDtypeStruct((b, s, w), BF16) for w in (dk, dk, dv, dv)]
                  + [jax.ShapeDtypeStruct((b, s, dk), F32)],
        scratch_shapes=[pltpu.VMEM((d, main + LANES), BF16), pltpu.VMEM((LANES, dk), BF16)],
        compiler_params=_params("arbitrary", "arbitrary"),
        name="gla_in_proj",
    )(x, norm_g, mod, mod, w_in, w_gate2, b_gate.astype(F32).reshape(1, dk))


def _gla_core_kernel(q_ref, k_ref, v_ref, g_ref, la_ref, x_ref, wf32_ref, gate_ref, fg_ref, o_ref,
                     state_ref, y_ref, w_ref, *, heads, hk, hv, nchunk, final):
    _cast_once(wf32_ref, w_ref)

    @pl.when(pl.program_id(1) == 0)
    def _():
        state_ref[...] = jnp.zeros_like(state_ref)

    tile = nchunk * CHUNK
    row = lax.broadcasted_iota(jnp.int32, (tile, tile), 0)
    col = lax.broadcasted_iota(jnp.int32, (tile, tile), 1)
    same = (row // CHUNK) == (col // CHUNK)
    causal = row >= col
    tri = jnp.where(jnp.logical_and(same, causal), 1.0, 0.0).astype(BF16)
    scale = hk ** -0.5
    dk = heads * hk
    chunk_rows = [slice(c * CHUNK, (c + 1) * CHUNK) for c in range(nchunk)]

    def run_tile(rows):
        cb = _tri_cumsum(tri, la_ref[rows, :], pieces=2)
        lasts = [cb[r.stop - 1:r.stop, :] for r in chunk_rows]
        cb_last = jnp.concatenate([jnp.broadcast_to(last, (CHUNK, dk)) for last in lasts], axis=0)
        eb = jnp.exp(cb)
        enb = jnp.exp(-cb)
        qf = q_ref[rows, :].astype(F32) * scale
        kf = k_ref[rows, :].astype(F32)
        q_e = (qf * eb).astype(BF16)
        q_n = (qf * enb).astype(BF16)
        k_n = (kf * enb).astype(BF16)
        k_e = (kf * eb).astype(BF16)
        k_in = (kf * jnp.exp(cb_last - cb)).astype(BF16)
        for h in range(heads):
            ks = slice(h * hk, (h + 1) * hk)
            vs = slice(h * hv, (h + 1) * hv)
            v = v_ref[rows, vs]
            a_causal = lax.dot_general(q_e[:, ks], k_n[:, ks], NT_DIMS, preferred_element_type=F32)
            a_anti = lax.dot_general(q_n[:, ks], k_e[:, ks], NT_DIMS, preferred_element_type=F32)
            attn = jnp.where(same, jnp.where(causal, a_causal, a_anti), 0.0).astype(BF16)
            updates = [lax.dot_general(k_in[r, ks], v[r, :], TN_DIMS, preferred_element_type=F32)
                       for r in chunk_rows]
            o = _dot(attn, v)
            state = state_ref[h]
            carried = []
            for c in range(nchunk):
                carried.append(state.astype(BF16))
                decay_col = jnp.broadcast_to(jnp.exp(lasts[c][:, ks]), (hk, hk)).T
                state = state * jnp.concatenate([decay_col] * (hv // hk), axis=1) + updates[c]
            state_ref[h] = state
            o = o + jnp.concatenate([_dot(q_e[r, ks], carried[c]) for c, r in enumerate(chunk_rows)], axis=0)
            ms = jnp.mean(o * o, axis=-1, keepdims=True)
            gate = _silu(g_ref[rows, vs].astype(F32))
            y_ref[rows, vs] = (o * lax.rsqrt(ms + EPS) * gate).astype(BF16)

    for t0 in range(0, q_ref.shape[0], tile):
        run_tile(slice(t0, t0 + tile))
    _residual_out(y_ref, w_ref, x_ref, gate_ref, fg_ref, o_ref, final)


def _gla_core(q, k, v, g, la, x, w_out, j, mod, layer, final_g, final, tile=256, tl=512):
    b, s, dk = q.shape
    dv = v.shape[-1]
    d = x.shape[-1]
    heads = GLA_HEADS
    hk, hv = dk // heads, dv // heads
    _, modspec = _layer_vec_specs(layer, d)
    tok = lambda w: pl.BlockSpec((None, tl, w), lambda bi, l: (bi, l, 0))
    return pl.pallas_call(
        functools.partial(_gla_core_kernel, heads=heads, hk=hk, hv=hv, nchunk=tile // CHUNK, final=final),
        grid=(b, s // tl),
        in_specs=[tok(dk), tok(dk), tok(dv), tok(dv), tok(dk), tok(d), _resident((dv, d), j), modspec(2),
                  _resident((1, d))],
        out_specs=tok(d),
        out_shape=jax.ShapeDtypeStruct((b, s, d), F32),
        scratch_shapes=[pltpu.VMEM((heads, hk, hv), F32), pltpu.VMEM((tl, dv), BF16), pltpu.VMEM((dv, d), BF16)],
        compiler_params=_params("arbitrary", "arbitrary"),
        name="gla_core_out",
    )(q, k, v, g, la, x, w_out, mod, final_g)


def _out_kernel(y_ref, wf32_ref, x_ref, gate_ref, fg_ref, o_ref, w_ref, *, final):
    _cast_once(wf32_ref, w_ref)
    _residual_out(y_ref, w_ref, x_ref, gate_ref, fg_ref, o_ref, final)


def _out_proj(y, w_out, j, x, mod, layer, final_g, final, tm=512):
    b, s, d = x.shape
    kdim = y.shape[-1]
    _, modspec = _layer_vec_specs(layer, d)
    tok = lambda w: pl.BlockSpec((None, tm, w), lambda bi, m: (bi, m, 0))
    return pl.pallas_call(
        functools.partial(_out_kernel, final=final),
        grid=(b, s // tm),
        in_specs=[tok(kdim), _resident((kdim, d), j), tok(d), modspec(2), _resident((1, d))],
        out_specs=tok(d),
        out_shape=jax.ShapeDtypeStruct((b, s, d), F32),
        scratch_shapes=[pltpu.VMEM((kdim, d), BF16)],
        compiler_params=_params("arbitrary", "arbitrary"),
        name="out_proj_residual",
    )(y, w_out, x, mod, final_g)


def kernel(x, c, positions, mod_w, mod_b, norm_g, ret_w_in, ret_w_out, fox_w_in, fox_b_f, fox_q_gain,
           fox_k_gain, fox_w_out, gla_w_in, gla_w_gate2, gla_b_gate, gla_w_out, final_g):
    depth, d, _ = mod_w.shape
    b = x.shape[0]
    mod = _modulation(c, mod_w, mod_b).reshape(depth, b, 3, 1, d)
    ng = norm_g.astype(F32).reshape(depth, 1, d)
    fg = final_g.astype(F32).reshape(1, d)
    cos, sin = _rope_tables(positions, d // RET_HEADS // 2)
    ret_w_in_bf = ret_w_in.astype(BF16)
    for i in range(depth):
        j = i // N_MIXERS
        kind = i % N_MIXERS
        final = i == depth - 1
        if kind == 0:
            q, k, v, g = _ret_in_proj(x, cos, sin, ng, mod, i, ret_w_in_bf, j)
            x = _ret_core(q, k, v, g, x, ret_w_out, j, mod, i, fg, final)
        elif kind == 1:
            q, k, v, g, cum, cum_first, cum_last = _fox_in_proj(x, ng, mod, i, fox_w_in, j, fox_b_f[j])
            y = _fox_flash(q, k, v, g, cum, cum_first, cum_last, fox_q_gain[j], fox_k_gain[j])
            x = _out_proj(y, fox_w_out, j, x, mod, i, fg, final)
        else:
            q, k, v, g, la = _gla_in_proj(x, ng, mod, i, gla_w_in, gla_w_gate2, j, gla_b_gate[j])
            x = _gla_core(q, k, v, g, la, x, gla_w_out, j, mod, i, fg, final)
    return x
```

```python
import functools
import math

import jax
import jax.numpy as jnp
from jax import lax
from jax.experimental import pallas as pl
from jax.experimental.pallas import tpu as pltpu

F32 = jnp.float32
BF16 = jnp.bfloat16

EPS = 1e-6
N_MIXERS = 3
CHUNK = 64
RET_HEADS = 4
ROPE_BASE = 10000.0
FOX_HEADS = 16
GLA_HEADS = 4
GLA_RANK = 16
GLA_TAU = 16.0

LANES = 128
RET_SUPER = 256
NEG_BIG = -1e30
LOG2E = math.log2(math.e)
FOX_BOUND_MARGIN = 1.02
FOX_BOUND_MAX = 48.0
FOX_SKIP_BITS = 160.0
VMEM_LIMIT = 56 * 1024 * 1024

NT_DIMS = (((1,), (1,)), ((), ()))
TN_DIMS = (((0,), (0,)), ((), ()))


def _silu(x):
    return x * (1.0 / (1.0 + jnp.exp(-x)))


def _log_sigmoid(z):
    return jnp.minimum(z, 0.0) - jnp.log1p(jnp.exp(-jnp.abs(z)))


def _dot(a, b):
    return jnp.dot(a, b, preferred_element_type=F32)


def _split3(x):
    hi = x.astype(BF16)
    r1 = x - hi.astype(F32)
    mid = r1.astype(BF16)
    lo = (r1 - mid.astype(F32)).astype(BF16)
    return hi, mid, lo


def _tri_cumsum(tri, x, pieces=3):
    parts = _split3(x)[:pieces]
    out = _dot(tri, parts[0])
    for part in parts[1:]:
        out = out + _dot(tri, part)
    return out


def _lower_tri(n):
    row = lax.broadcasted_iota(jnp.int32, (n, n), 0)
    col = lax.broadcasted_iota(jnp.int32, (n, n), 1)
    return row >= col


def _norm_mod(x_ref, ng_ref, sc_ref, sh_ref):
    x = x_ref[...]
    ms = jnp.mean(x * x, axis=-1, keepdims=True)
    y = x * lax.rsqrt(ms + EPS) * ng_ref[...]
    return (y * (1.0 + sc_ref[...]) + sh_ref[...]).astype(BF16)


def _project(h, w_ref, col0, width, o_ref, tn):
    for n0 in range(0, width, tn):
        o_ref[:, n0:n0 + tn] = _dot(h, w_ref[:, col0 + n0:col0 + n0 + tn]).astype(o_ref.dtype)


def _cast_once(w_ref, wbf_ref, tn=512):
    @pl.when(jnp.logical_and(pl.program_id(0) == 0, pl.program_id(1) == 0))
    def _():
        cols = w_ref.shape[1]
        aligned = cols // LANES * LANES
        for n0 in range(0, aligned, tn):
            n1 = min(n0 + tn, aligned)
            wbf_ref[:, n0:n1] = w_ref[:, n0:n1].astype(BF16)
        if aligned < cols:
            wbf_ref[:, aligned:] = jnp.zeros((w_ref.shape[0], wbf_ref.shape[1] - aligned), BF16)
            wbf_ref[:, aligned:cols] = w_ref[:, aligned:cols].astype(BF16)


def _params(*sem):
    return pltpu.CompilerParams(dimension_semantics=sem, vmem_limit_bytes=VMEM_LIMIT)


def _resident(shape, layer=None):
    nd = len(shape)
    if layer is None:
        return pl.BlockSpec(shape, lambda *_: (0,) * nd, pipeline_mode=pl.Buffered(1))
    return pl.BlockSpec((None,) + tuple(shape), lambda *_: (layer,) + (0,) * nd, pipeline_mode=pl.Buffered(1))


def _mod_kernel(c_ref, w_ref, b_ref, o_ref):
    a = _silu(c_ref[...])
    o_ref[...] = _dot(a.astype(BF16), w_ref[...].astype(BF16)) + b_ref[...]


def _modulation(c, mod_w, mod_b, tn=1536):
    depth, d, d3 = mod_w.shape
    b = c.shape[0]
    return pl.pallas_call(
        _mod_kernel,
        grid=(depth, d3 // tn),
        in_specs=[
            pl.BlockSpec((b, d), lambda i, n: (0, 0)),
            pl.BlockSpec((None, d, tn), lambda i, n: (i, 0, n)),
            pl.BlockSpec((None, 1, tn), lambda i, n: (i, 0, n)),
        ],
        out_specs=pl.BlockSpec((None, b, tn), lambda i, n: (i, 0, n)),
        out_shape=jax.ShapeDtypeStruct((depth, b, d3), F32),
        compiler_params=_params("parallel", "parallel"),
        name="adaln_modulation",
    )(c, mod_w, mod_b.reshape(depth, 1, d3))


def _layer_vec_specs(layer, d):
    ng = pl.BlockSpec((None, 1, d), lambda b, m: (layer, 0, 0))

    def mod(which):
        return pl.BlockSpec((None, None, None, 1, d), lambda b, m: (layer, b, which, 0, 0))
    return ng, mod


def _rope_kernel(pos_ref, inv_ref, cos_ref, sin_ref):
    ang = pos_ref[...].astype(F32) * inv_ref[...]
    cos_ref[...] = jnp.cos(ang)
    sin_ref[...] = jnp.sin(ang)


def _rope_tables(positions, half, tm=2048):
    b, s = positions.shape
    inv = (ROPE_BASE ** (-jnp.arange(half, dtype=F32) / half)).reshape(1, half)
    tok = lambda w: pl.BlockSpec((None, tm, w), lambda bi, m: (bi, m, 0))
    return pl.pallas_call(
        _rope_kernel,
        grid=(b, s // tm),
        in_specs=[tok(1), pl.BlockSpec((1, half), lambda bi, m: (0, 0))],
        out_specs=[tok(half), tok(half)],
        out_shape=[jax.ShapeDtypeStruct((b, s, half), F32)] * 2,
        compiler_params=_params("parallel", "parallel"),
        name="rope_tables",
    )(positions.reshape(b, s, 1), inv)


def _ret_in_kernel(x_ref, ng_ref, sc_ref, sh_ref, cos_ref, sin_ref, w_ref,
                   q_ref, k_ref, v_ref, g_ref, *, heads, hk, dv, tn):
    h = _norm_mod(x_ref, ng_ref, sc_ref, sh_ref)
    cos = cos_ref[...]
    sin = sin_ref[...]
    half = hk // 2
    dk = heads * hk
    _project(h, w_ref, 2 * dk, dv, v_ref, tn)
    for col0, o_ref, scale in ((0, q_ref, hk ** -0.5), (dk, k_ref, None)):
        for hd in range(heads):
            r = _dot(h, w_ref[:, col0 + hd * hk:col0 + (hd + 1) * hk])
            x1 = r[:, :half]
            x2 = r[:, half:]
            o1 = x1 * cos - x2 * sin
            o2 = x1 * sin + x2 * cos
            if scale is not None:
                o1 = o1 * scale
                o2 = o2 * scale
            o_ref[:, hd * hk:hd * hk + half] = o1.astype(BF16)
            o_ref[:, hd * hk + half:(hd + 1) * hk] = o2.astype(BF16)
    _project(h, w_ref, 2 * dk + dv, dv, g_ref, tn)


def _ret_in_proj(x, cos, sin, norm_g, mod, layer, w_in, j, tm=512, tn=512):
    b, s, d = x.shape
    heads = RET_HEADS
    dk = d
    dv = 2 * d
    hk = dk // heads
    n_in = w_in.shape[-1]
    half = hk // 2
    ng, modspec = _layer_vec_specs(layer, d)
    tok = lambda w: pl.BlockSpec((None, tm, w), lambda bi, m: (bi, m, 0))
    return pl.pallas_call(
        functools.partial(_ret_in_kernel, heads=heads, hk=hk, dv=dv, tn=tn),
        grid=(b, s // tm),
        in_specs=[tok(d), ng, modspec(1), modspec(0), tok(half), tok(half), _resident((d, n_in), j)],
        out_specs=[tok(dk), tok(dk), tok(dv), tok(dv)],
        out_shape=[jax.ShapeDtypeStruct((b, s, w), BF16) for w in (dk, dk, dv, dv)],
        compiler_params=_params("parallel", "parallel"),
        name="retention_in_proj",
    )(x, norm_g, mod, mod, cos, sin, w_in)


def _residual_out(y_ref, w_ref, x_ref, gate_ref, fg_ref, o_ref, final):
    kdim = y_ref.shape[-1]
    half = kdim // 2
    r = _dot(y_ref[:, :half], w_ref[:half, :]) + _dot(y_ref[:, half:], w_ref[half:, :])
    xn = x_ref[...] + gate_ref[...] * r
    if final:
        ms = jnp.mean(xn * xn, axis=-1, keepdims=True)
        xn = xn * lax.rsqrt(ms + EPS) * fg_ref[...]
    o_ref[...] = xn


def _ret_core_kernel(q_ref, k_ref, v_ref, g_ref, x_ref, wf32_ref, gate_ref, fg_ref, o_ref,
                     state_ref, dmask_ref, qdec_ref, kdec_ref, y_ref, w_ref, *, heads, hk, hv, sup, final):
    log_gamma = [math.log1p(-(2.0 ** (-5.0 - h))) for h in range(heads)]
    _cast_once(wf32_ref, w_ref)

    @pl.when(pl.program_id(1) == 0)
    def _():
        state_ref[...] = jnp.zeros_like(state_ref)
        t = lax.broadcasted_iota(jnp.int32, (sup, sup), 0)
        s = lax.broadcasted_iota(jnp.int32, (sup, sup), 1)
        dist = jnp.abs(t - s).astype(F32)
        visible = (s // CHUNK) <= (t // CHUNK)
        pos = lax.broadcasted_iota(jnp.int32, (sup, hk), 0).astype(F32)
        for h in range(heads):
            dmask_ref[h] = jnp.where(visible, jnp.exp(log_gamma[h] * dist), 0.0)
            qdec_ref[h] = jnp.exp(log_gamma[h] * (pos + 1.0))
            kdec_ref[h] = jnp.exp(log_gamma[h] * (sup - 1.0 - pos))

    for r0 in range(0, q_ref.shape[0], sup):
        rows = slice(r0, r0 + sup)
        for h in range(heads):
            q = q_ref[rows, h * hk:(h + 1) * hk]
            k = k_ref[rows, h * hk:(h + 1) * hk]
            v = v_ref[rows, h * hv:(h + 1) * hv]
            scores = lax.dot_general(q, k, NT_DIMS, preferred_element_type=F32) * dmask_ref[h]
            state = state_ref[h]
            q_in = (q.astype(F32) * qdec_ref[h]).astype(BF16)
            o = _dot(scores.astype(BF16), v) + _dot(q_in, state.astype(BF16))
            k_in = (k.astype(F32) * kdec_ref[h]).astype(BF16)
            state_ref[h] = math.exp(log_gamma[h] * sup) * state + lax.dot_general(
                k_in, v, TN_DIMS, preferred_element_type=F32)
            mu = jnp.mean(o, axis=-1, keepdims=True)
            oc = o - mu
            var = jnp.mean(oc * oc, axis=-1, keepdims=True)
            gate = _silu(g_ref[rows, h * hv:(h + 1) * hv].astype(F32))
            y_ref[rows, h * hv:(h + 1) * hv] = (oc * lax.rsqrt(var + EPS) * gate).astype(BF16)
    _residual_out(y_ref, w_ref, x_ref, gate_ref, fg_ref, o_ref, final)


def _ret_core(q, k, v, g, x, w_out, j, mod, layer, final_g, final, tl=2 * RET_SUPER):
    b, s, dk = q.shape
    dv = v.shape[-1]
    d = x.shape[-1]
    heads = RET_HEADS
    hk, hv = dk // heads, dv // heads
    sup = RET_SUPER
    _, modspec = _layer_vec_specs(layer, d)
    tok = lambda w: pl.BlockSpec((None, tl, w), lambda bi, l: (bi, l, 0))
    return pl.pallas_call(
        functools.partial(_ret_core_kernel, heads=heads, hk=hk, hv=hv, sup=sup, final=final),
        grid=(b, s // tl),
        in_specs=[tok(dk), tok(dk), tok(dv), tok(dv), tok(d), _resident((dv, d), j), modspec(2), _resident((1, d))],
        out_specs=tok(d),
        out_shape=jax.ShapeDtypeStruct((b, s, d), F32),
        scratch_shapes=[pltpu.VMEM((heads, hk, hv), F32), pltpu.VMEM((heads, sup, sup), F32),
                        pltpu.VMEM((heads, sup, hk), F32), pltpu.VMEM((heads, sup, hk), F32),
                        pltpu.VMEM((tl, dv), BF16), pltpu.VMEM((dv, d), BF16)],
        compiler_params=_params("arbitrary", "arbitrary"),
        name="retention_core_out",
    )(q, k, v, g, x, w_out, mod, final_g)


def _fox_in_kernel(x_ref, ng_ref, sc_ref, sh_ref, wf32_ref, bf_ref,
                   q_ref, k_ref, v_ref, g_ref, cum_ref, first_ref, last_ref, carry_ref, w_ref, *, width, tn):
    _cast_once(wf32_ref, w_ref)

    @pl.when(pl.program_id(1) == 0)
    def _():
        carry_ref[...] = jnp.zeros_like(carry_ref)

    h = _norm_mod(x_ref, ng_ref, sc_ref, sh_ref)
    log_f = _log_sigmoid(_dot(h, w_ref[:, 4 * width:]) + bf_ref[...])
    _project(h, w_ref, 0, width, q_ref, tn)
    rows = log_f.shape[0]
    tri = jnp.where(_lower_tri(rows), 1.0, 0.0).astype(BF16)
    cum = _tri_cumsum(tri, log_f) + carry_ref[...]
    carry_ref[...] = cum[rows - 1:rows, :]
    cum_ref[...] = cum
    first_ref[...] = cum[0:1, :]
    last_ref[0:1, :] = cum[rows // 2 - 1:rows // 2, :]
    last_ref[1:2, :] = cum[rows - 1:rows, :]
    for idx, o_ref in ((1, k_ref), (2, v_ref), (3, g_ref)):
        _project(h, w_ref, idx * width, width, o_ref, tn)


def _fox_in_proj(x, norm_g, mod, layer, w_in, j, b_f, tm=512, tn=512):
    b, s, d = x.shape
    heads = FOX_HEADS
    n_in = w_in.shape[-1]
    width = (n_in - heads) // 4
    bias = jnp.pad(b_f.astype(F32), (0, LANES - heads)).reshape(1, LANES)
    ng, modspec = _layer_vec_specs(layer, d)
    tok = lambda w: pl.BlockSpec((None, tm, w), lambda bi, m: (bi, m, 0))
    edge = lambda n: pl.BlockSpec((None, None, n, LANES), lambda bi, m: (bi, m, 0, 0))
    return pl.pallas_call(
        functools.partial(_fox_in_kernel, width=width, tn=tn),
        grid=(b, s // tm),
        in_specs=[tok(d), ng, modspec(1), modspec(0), _resident((d, n_in), j), _resident((1, LANES))],
        out_specs=[tok(width)] * 4 + [tok(LANES), edge(1), edge(2)],
        out_shape=[jax.ShapeDtypeStruct((b, s, width), BF16)] * 4 + [jax.ShapeDtypeStruct((b, s, LANES), F32)]
                  + [jax.ShapeDtypeStruct((b, s // tm, n, LANES), F32) for n in (1, 2)],
        scratch_shapes=[pltpu.VMEM((1, LANES), F32), pltpu.VMEM((d, 4 * width + LANES), BF16)],
        compiler_params=_params("arbitrary", "arbitrary"),
        name="fox_in_proj",
    )(x, norm_g, mod, mod, w_in, bias)


def _fox_flash_kernel(*refs, bq, hd, seq, bounded):
    if bounded:
        (start_ref, q_ref, k_ref, v_ref, g_ref, cum_ref, qg_ref, kg_ref, bound_ref, y_ref,
         kx_ref, vx_ref, qx_ref, acc_ref) = refs
    else:
        (q_ref, k_ref, v_ref, g_ref, cum_ref, qg_ref, kg_ref, y_ref,
         kx_ref, vx_ref, qx_ref, acc_ref, m_ref) = refs
    pair = pl.program_id(1)
    lane = lax.broadcasted_iota(jnp.int32, (1, LANES), 1)
    first = lane < hd
    sels = (first, jnp.logical_not(first))
    aux0 = (hd, 0)

    def head_norm(x, gain):
        sq = x * x
        s0 = jnp.sum(jnp.where(first, sq, 0.0), axis=-1, keepdims=True)
        s1 = jnp.sum(jnp.where(first, 0.0, sq), axis=-1, keepdims=True)
        ms = jnp.where(first, s0, s1) * (1.0 / hd)
        return x * lax.rsqrt(ms + EPS) * gain

    def bias_lanes(cum, offset):
        src = lax.broadcasted_iota(jnp.int32, (LANES, LANES), 0)
        dst = lax.broadcasted_iota(jnp.int32, (LANES, LANES), 1)
        blocks = []
        for t in range(3):
            hit = jnp.logical_or(jnp.logical_and(src == 2 * pair, dst == aux0[0] + offset + t),
                                 jnp.logical_and(src == 2 * pair + 1, dst == aux0[1] + offset + t))
            blocks.append(jnp.where(hit, 1.0, 0.0))
        select = jnp.concatenate(blocks, axis=0).astype(BF16)
        moved = _dot(jnp.concatenate(_split3(cum), axis=1), select)
        out = []
        for j in range(2):
            one0 = aux0[j] + 3 - offset
            ones = jnp.where(jnp.logical_and(lane >= one0, lane < one0 + 3), 1.0, 0.0)
            out.append(moved + ones)
        return out

    one = jnp.ones((), BF16)
    for r0 in range(0, seq, bq):
        kn = head_norm(k_ref[r0:r0 + bq, :].astype(F32), kg_ref[...])
        bias = bias_lanes(cum_ref[r0:r0 + bq, :] * (-LOG2E), 0)
        v = v_ref[r0:r0 + bq, :]
        for j in range(2):
            kx_ref[j, r0:r0 + bq, :] = jnp.where(sels[j], kn, bias[j]).astype(BF16)
            vx_ref[j, r0:r0 + bq, :] = jnp.where(sels[j], v, one)

    def query_block(qi, carry):
        q0 = pl.multiple_of(qi * bq, bq)
        qrows = pl.ds(q0, bq)
        qn = head_norm(q_ref[qrows, :].astype(F32), qg_ref[...]) * (hd ** -0.5 * LOG2E)
        cum_q = cum_ref[qrows, :] * LOG2E
        if bounded:
            cum_q = cum_q - bound_ref[...]
        bias = bias_lanes(cum_q, 3)
        for j in range(2):
            qx_ref[j] = jnp.where(sels[j], qn, bias[j]).astype(BF16)
        acc_ref[...] = jnp.zeros_like(acc_ref)
        if not bounded:
            m_ref[...] = jnp.full_like(m_ref, NEG_BIG)
        causal = _lower_tri(bq)

        def bounded_blocks(full, diagonal, lead_half=None):
            half = bq // 2
            work = []
            if lead_half is not None:
                rk = pl.ds(pl.multiple_of(lead_half * half, half), half)
                work += [(j, slice(0, bq), rk, None) for j in range(2)]
            for kb in full:
                r0 = pl.multiple_of(kb * bq, bq)
                work += [(j, slice(0, bq), pl.ds(r0, bq), None) for j in range(2)]
            if diagonal:
                for j in range(2):
                    work.append((j, slice(0, half), pl.ds(q0, half), causal[:half, :half]))
                    work.append((j, slice(half, bq), qrows, causal[half:, :]))
            logits = [lax.dot_general(qx_ref[j, rq, :], kx_ref[j, rk, :], NT_DIMS, preferred_element_type=F32)
                      for j, rq, rk, _ in work]
            add = {}
            for (j, rq, rk, mask), s in zip(work, logits):
                if mask is not None:
                    s = jnp.where(mask, s, NEG_BIG)
                pv = _dot(jnp.exp2(s).astype(BF16), vx_ref[j, rk, :])
                key = (j, rq.start, rq.stop)
                add[key] = pv if key not in add else add[key] + pv
            for (j, r0, r1), pv in add.items():
                acc_ref[j, r0:r1, :] += pv

        def step(kb, masked):
            rk = pl.ds(pl.multiple_of(kb * bq, bq), bq)
            logits = [lax.dot_general(qx_ref[j], kx_ref[j, rk, :], NT_DIMS, preferred_element_type=F32)
                      for j in range(2)]
            for j in range(2):
                s = jnp.where(causal, logits[j], NEG_BIG) if masked else logits[j]
                m_prev = m_ref[j]
                m_new = jnp.maximum(m_prev, jnp.max(s, axis=-1, keepdims=True))
                p = jnp.exp2(s - jnp.concatenate([m_new] * (bq // LANES), axis=1))
                acc_ref[j] = jnp.exp2(m_prev - m_new) * acc_ref[j] + _dot(p.astype(BF16), vx_ref[j, rk, :])
                m_ref[j] = m_new

        if bounded:
            first_half = start_ref[(pl.program_id(0) * pl.num_programs(1) + pair) * (seq // bq) + qi]
            lead = jnp.bitwise_and(first_half, 1) == 1
            start = jnp.right_shift(first_half + 1, 1)
            count = qi - start

            def body(t, carry):
                bounded_blocks([start + 2 * t, start + 2 * t + 1], False)
                return carry

            lax.fori_loop(0, jnp.right_shift(count, 1), body, 0)
            odd = jnp.bitwise_and(count, 1) == 1
            for is_odd in (True, False):
                for has_lead in (True, False):
                    @pl.when(jnp.logical_and(odd == is_odd, lead == has_lead))
                    def _():
                        bounded_blocks([qi - 1] if is_odd else [], True, first_half if has_lead else None)
        else:
            def body(kb, carry):
                step(kb, False)
                return carry

            lax.fori_loop(0, qi, body, 0)
            step(qi, True)
        num = jnp.where(first, acc_ref[0], acc_ref[1])
        den = jnp.where(first, pltpu.roll(acc_ref[0], hd, axis=1), pltpu.roll(acc_ref[1], hd, axis=1))
        y_ref[qrows, :] = (num / den * _silu(g_ref[qrows, :].astype(F32))).astype(BF16)
        return carry

    lax.fori_loop(0, seq // bq, query_block, 0)


def _fox_flash(q, k, v, g, cum, cum_first, cum_last, q_gain, k_gain, bq=512):
    b, s, width = q.shape
    assert cum_first.shape[1] == s // bq and cum_last.shape[2] == 2
    heads = FOX_HEADS
    hd = width // heads
    assert 2 * hd == LANES
    pairs = heads // 2
    qg = jnp.tile(q_gain.astype(F32), 2).reshape(1, LANES)
    kg = jnp.tile(k_gain.astype(F32), 2).reshape(1, LANES)
    nq = s // bq
    full = pl.BlockSpec((None, s, LANES), lambda bi, p, *_: (bi, 0, p))
    row = pl.BlockSpec((1, LANES), lambda bi, p, *_: (0, 0))

    def call(bounded, *extra):
        grid_spec = pltpu.PrefetchScalarGridSpec(
            num_scalar_prefetch=1 if bounded else 0,
            grid=(b, pairs),
            in_specs=[full, full, full, full, pl.BlockSpec((None, s, LANES), lambda bi, p, *_: (bi, 0, 0)),
                      row, row] + ([row] if bounded else []),
            out_specs=full,
            scratch_shapes=[pltpu.VMEM((2, s, LANES), BF16), pltpu.VMEM((2, s, LANES), BF16),
                            pltpu.VMEM((2, bq, LANES), BF16), pltpu.VMEM((2, bq, LANES), F32)]
                           + ([] if bounded else [pltpu.VMEM((2, bq, LANES), F32)]))
        return pl.pallas_call(
            functools.partial(_fox_flash_kernel, bq=bq, hd=hd, seq=s, bounded=bounded),
            grid_spec=grid_spec,
            out_shape=jax.ShapeDtypeStruct((b, s, width), BF16),
            compiler_params=_params("parallel", "parallel"),
            name="fox_flash_bounded" if bounded else "fox_flash_online",
        )(*extra[:1], q, k, v, g, cum, qg, kg, *extra[1:])

    def first_key_blocks():
        at_first = cum_first[:, :, 0, :heads]
        at_last = cum_last[:, :, :, :heads].reshape(b, 2 * nq, heads)
        gap = at_first[:, :, None, :] - at_last[:, None, :, :]
        blocks = jnp.arange(nq, dtype=jnp.int32)
        halves = jnp.arange(2 * nq, dtype=jnp.int32)
        earlier = (halves[None, :] < 2 * blocks[:, None])[None, :, :, None]
        needed = jnp.logical_and(gap * LOG2E > -FOX_SKIP_BITS, earlier)
        first = jnp.where(jnp.any(needed, axis=2), jnp.argmax(needed, axis=2).astype(jnp.int32),
                          2 * blocks[None, :, None])
        first = jnp.min(first.reshape(b, nq, pairs, 2), axis=-1)
        return jnp.transpose(first, (0, 2, 1)).reshape(-1)

    bound = FOX_BOUND_MARGIN * LOG2E * hd ** 0.5 * jnp.max(jnp.abs(qg)) * jnp.max(jnp.abs(kg))
    return lax.cond(bound <= FOX_BOUND_MAX,
                    lambda: call(True, first_key_blocks(), jnp.full((1, LANES), bound, F32)),
                    lambda: call(False))


def _gla_in_kernel(x_ref, ng_ref, sc_ref, sh_ref, wf32_ref, wg2f32_ref, bg_ref,
                   q_ref, k_ref, v_ref, g_ref, la_ref, w_ref, wg2_ref, *, dk, dv, tn):
    _cast_once(wf32_ref, w_ref)

    @pl.when(jnp.logical_and(pl.program_id(0) == 0, pl.program_id(1) == 0))
    def _():
        wg2_ref[...] = jnp.zeros_like(wg2_ref)
        wg2_ref[:wg2f32_ref.shape[0], :] = wg2f32_ref[...].astype(BF16)

    h = _norm_mod(x_ref, ng_ref, sc_ref, sh_ref)
    main = 2 * dk + 2 * dv
    r = _dot(h, w_ref[:, main:])
    _project(h, w_ref, 0, dk, q_ref, tn)
    z = _dot(r.astype(BF16), wg2_ref[...]) + bg_ref[...]
    _project(h, w_ref, dk, dk, k_ref, tn)
    la_ref[...] = _log_sigmoid(z) * (1.0 / GLA_TAU)
    _project(h, w_ref, 2 * dk, dv, v_ref, tn)
    _project(h, w_ref, 2 * dk + dv, dv, g_ref, tn)


def _gla_in_proj(x, norm_g, mod, layer, w_in, w_gate2, j, b_gate, tm=512, tn=512):
    b, s, d = x.shape
    dk, dv, rank = d // 2, d, GLA_RANK
    main = 2 * dk + 2 * dv
    ng, modspec = _layer_vec_specs(layer, d)
    tok = lambda w, : pl.BlockSpec((None, tm, w), lambda bi, m: (bi, m, 0))
    return pl.pallas_call(
        functools.partial(_gla_in_kernel, dk=dk, dv=dv, tn=tn),
        grid=(b, s // tm),
        in_specs=[tok(d), ng, modspec(1), modspec(0), _resident((d, main + rank), j),
                  _resident((rank, dk), j), _resident((1, dk))],
        out_specs=[tok(dk), tok(dk), tok(dv), tok(dv), tok(dk)],
        out_shape=[jax.ShapeDtypeStruct((b, s, w), BF16) for w in (dk, dk, dv, dv)]
                  + [jax.ShapeDtypeStruct((b, s, dk), F32)],
        scratch_shapes=[pltpu.VMEM((d, main + LANES), BF16), pltpu.VMEM((LANES, dk), BF16)],
        compiler_params=_params("arbitrary", "arbitrary"),
        name="gla_in_proj",
    )(x, norm_g, mod, mod, w_in, w_gate2, b_gate.astype(F32).reshape(1, dk))


def _gla_core_kernel(q_ref, k_ref, v_ref, g_ref, la_ref, x_ref, wf32_ref, gate_ref, fg_ref, o_ref,
                     state_ref, y_ref, w_ref, *, heads, hk, hv, nchunk, final):
    _cast_once(wf32_ref, w_ref)

    @pl.when(pl.program_id(1) == 0)
    def _():
        state_ref[...] = jnp.zeros_like(state_ref)

    tile = nchunk * CHUNK
    row = lax.broadcasted_iota(jnp.int32, (tile, tile), 0)
    col = lax.broadcasted_iota(jnp.int32, (tile, tile), 1)
    same = (row // CHUNK) == (col // CHUNK)
    causal = row >= col
    tri = jnp.where(jnp.logical_and(same, causal), 1.0, 0.0).astype(BF16)
    scale = hk ** -0.5
    dk = heads * hk
    chunk_rows = [slice(c * CHUNK, (c + 1) * CHUNK) for c in range(nchunk)]

    def run_tile(rows):
        cb = _tri_cumsum(tri, la_ref[rows, :], pieces=2)
        lasts = [cb[r.stop - 1:r.stop, :] for r in chunk_rows]
        cb_last = jnp.concatenate([jnp.broadcast_to(last, (CHUNK, dk)) for last in lasts], axis=0)
        eb = jnp.exp(cb)
        enb = jnp.exp(-cb)
        qf = q_ref[rows, :].astype(F32) * scale
        kf = k_ref[rows, :].astype(F32)
        q_e = (qf * eb).astype(BF16)
        q_n = (qf * enb).astype(BF16)
        k_n = (kf * enb).astype(BF16)
        k_e = (kf * eb).astype(BF16)
        k_in = (kf * jnp.exp(cb_last - cb)).astype(BF16)
        for h in range(heads):
            ks = slice(h * hk, (h + 1) * hk)
            vs = slice(h * hv, (h + 1) * hv)
            v = v_ref[rows, vs]
            a_causal = lax.dot_general(q_e[:, ks], k_n[:, ks], NT_DIMS, preferred_element_type=F32)
            a_anti = lax.dot_general(q_n[:, ks], k_e[:, ks], NT_DIMS, preferred_element_type=F32)
            attn = jnp.where(same, jnp.where(causal, a_causal, a_anti), 0.0).astype(BF16)
            updates = [lax.dot_general(k_in[r, ks], v[r, :], TN_DIMS, preferred_element_type=F32)
                       for r in chunk_rows]
            o = _dot(attn, v)
            state = state_ref[h]
            carried = []
            for c in range(nchunk):
                carried.append(state.astype(BF16))
                decay_col = jnp.broadcast_to(jnp.exp(lasts[c][:, ks]), (hk, hk)).T
                state = state * jnp.concatenate([decay_col] * (hv // hk), axis=1) + updates[c]
            state_ref[h] = state
            o = o + jnp.concatenate([_dot(q_e[r, ks], carried[c]) for c, r in enumerate(chunk_rows)], axis=0)
            ms = jnp.mean(o * o, axis=-1, keepdims=True)
            gate = _silu(g_ref[rows, vs].astype(F32))
            y_ref[rows, vs] = (o * lax.rsqrt(ms + EPS) * gate).astype(BF16)

    for t0 in range(0, q_ref.shape[0], tile):
        run_tile(slice(t0, t0 + tile))
    _residual_out(y_ref, w_ref, x_ref, gate_ref, fg_ref, o_ref, final)


def _gla_core(q, k, v, g, la, x, w_out, j, mod, layer, final_g, final, tile=256, tl=1024):
    b, s, dk = q.shape
    dv = v.shape[-1]
    d = x.shape[-1]
    heads = GLA_HEADS
    hk, hv = dk // heads, dv // heads
    _, modspec = _layer_vec_specs(layer, d)
    tok = lambda w: pl.BlockSpec((None, tl, w), lambda bi, l: (bi, l, 0))
    return pl.pallas_call(
        functools.partial(_gla_core_kernel, heads=heads, hk=hk, hv=hv, nchunk=tile // CHUNK, final=final),
        grid=(b, s // tl),
        in_specs=[tok(dk), tok(dk), tok(dv), tok(dv), tok(dk), tok(d), _resident((dv, d), j), modspec(2),
                  _resident((1, d))],
        out_specs=tok(d),
        out_shape=jax.ShapeDtypeStruct((b, s, d), F32),
        scratch_shapes=[pltpu.VMEM((heads, hk, hv), F32), pltpu.VMEM((tl, dv), BF16), pltpu.VMEM((dv, d), BF16)],
        compiler_params=_params("arbitrary", "arbitrary"),
        name="gla_core_out",
    )(q, k, v, g, la, x, w_out, mod, final_g)


def _out_kernel(y_ref, wf32_ref, x_ref, gate_ref, fg_ref, o_ref, w_ref, *, final):
    _cast_once(wf32_ref, w_ref)
    _residual_out(y_ref, w_ref, x_ref, gate_ref, fg_ref, o_ref, final)


def _out_proj(y, w_out, j, x, mod, layer, final_g, final, tm=1024):
    b, s, d = x.shape
    kdim = y.shape[-1]
    _, modspec = _layer_vec_specs(layer, d)
    tok = lambda w: pl.BlockSpec((None, tm, w), lambda bi, m: (bi, m, 0))
    return pl.pallas_call(
        functools.partial(_out_kernel, final=final),
        grid=(b, s // tm),
        in_specs=[tok(kdim), _resident((kdim, d), j), tok(d), modspec(2), _resident((1, d))],
        out_specs=tok(d),
        out_shape=jax.ShapeDtypeStruct((b, s, d), F32),
        scratch_shapes=[pltpu.VMEM((kdim, d), BF16)],
        compiler_params=_params("arbitrary", "arbitrary"),
        name="out_proj_residual",
    )(y, w_out, x, mod, final_g)


def kernel(x, c, positions, mod_w, mod_b, norm_g, ret_w_in, ret_w_out, fox_w_in, fox_b_f, fox_q_gain,
           fox_k_gain, fox_w_out, gla_w_in, gla_w_gate2, gla_b_gate, gla_w_out, final_g):
    depth, d, _ = mod_w.shape
    b = x.shape[0]
    mod = _modulation(c, mod_w, mod_b).reshape(depth, b, 3, 1, d)
    ng = norm_g.astype(F32).reshape(depth, 1, d)
    fg = final_g.astype(F32).reshape(1, d)
    cos, sin = _rope_tables(positions, d // RET_HEADS // 2)
    ret_w_in_bf = ret_w_in.astype(BF16)
    for i in range(depth):
        j = i // N_MIXERS
        kind = i % N_MIXERS
        final = i == depth - 1
        if kind == 0:
            q, k, v, g = _ret_in_proj(x, cos, sin, ng, mod, i, ret_w_in_bf, j)
            x = _ret_core(q, k, v, g, x, ret_w_out, j, mod, i, fg, final)
        elif kind == 1:
            q, k, v, g, cum, cum_first, cum_last = _fox_in_proj(x, ng, mod, i, fox_w_in, j, fox_b_f[j])
            y = _fox_flash(q, k, v, g, cum, cum_first, cum_last, fox_q_gain[j], fox_k_gain[j])
            x = _out_proj(y, fox_w_out, j, x, mod, i, fg, final)
        else:
            q, k, v, g, la = _gla_in_proj(x, ng, mod, i, gla_w_in, gla_w_gate2, j, gla_b_gate[j])
            x = _gla_core(q, k, v, g, la, x, gla_w_out, j, mod, i, fg, final)
    return x
```

```python
import functools
import math

import jax
import jax.numpy as jnp
from jax import lax
from jax.experimental import pallas as pl
from jax.experimental.pallas import tpu as pltpu

F32 = jnp.float32
BF16 = jnp.bfloat16

EPS = 1e-6
N_MIXERS = 3
CHUNK = 64
RET_HEADS = 4
ROPE_BASE = 10000.0
FOX_HEADS = 16
GLA_HEADS = 4
GLA_RANK = 16
GLA_TAU = 16.0

LANES = 128
RET_SUPER = 256
NEG_BIG = -1e30
LOG2E = math.log2(math.e)
FOX_BOUND_MARGIN = 1.02
FOX_BOUND_MAX = 48.0
FOX_SKIP_BITS = 160.0
VMEM_LIMIT = 56 * 1024 * 1024

NT_DIMS = (((1,), (1,)), ((), ()))
TN_DIMS = (((0,), (0,)), ((), ()))


def _silu(x):
    return x * (1.0 / (1.0 + jnp.exp(-x)))


def _log_sigmoid(z):
    return jnp.minimum(z, 0.0) - jnp.log1p(jnp.exp(-jnp.abs(z)))


def _dot(a, b):
    return jnp.dot(a, b, preferred_element_type=F32)


def _split3(x):
    hi = x.astype(BF16)
    r1 = x - hi.astype(F32)
    mid = r1.astype(BF16)
    lo = (r1 - mid.astype(F32)).astype(BF16)
    return hi, mid, lo


def _tri_cumsum(tri, x, pieces=3):
    parts = _split3(x)[:pieces]
    out = _dot(tri, parts[0])
    for part in parts[1:]:
        out = out + _dot(tri, part)
    return out


def _lower_tri(n):
    row = lax.broadcasted_iota(jnp.int32, (n, n), 0)
    col = lax.broadcasted_iota(jnp.int32, (n, n), 1)
    return row >= col


def _norm_mod(x_ref, ng_ref, sc_ref, sh_ref):
    x = x_ref[...]
    ms = jnp.mean(x * x, axis=-1, keepdims=True)
    y = x * lax.rsqrt(ms + EPS) * ng_ref[...]
    return (y * (1.0 + sc_ref[...]) + sh_ref[...]).astype(BF16)


def _project(h, w_ref, col0, width, o_ref, tn):
    for n0 in range(0, width, tn):
        o_ref[:, n0:n0 + tn] = _dot(h, w_ref[:, col0 + n0:col0 + n0 + tn]).astype(o_ref.dtype)


def _cast_once(w_ref, wbf_ref, tn=512):
    @pl.when(jnp.logical_and(pl.program_id(0) == 0, pl.program_id(1) == 0))
    def _():
        cols = w_ref.shape[1]
        aligned = cols // LANES * LANES
        for n0 in range(0, aligned, tn):
            n1 = min(n0 + tn, aligned)
            wbf_ref[:, n0:n1] = w_ref[:, n0:n1].astype(BF16)
        if aligned < cols:
            wbf_ref[:, aligned:] = jnp.zeros((w_ref.shape[0], wbf_ref.shape[1] - aligned), BF16)
            wbf_ref[:, aligned:cols] = w_ref[:, aligned:cols].astype(BF16)


def _params(*sem):
    return pltpu.CompilerParams(dimension_semantics=sem, vmem_limit_bytes=VMEM_LIMIT)


def _resident(shape, layer=None):
    nd = len(shape)
    if layer is None:
        return pl.BlockSpec(shape, lambda *_: (0,) * nd, pipeline_mode=pl.Buffered(1))
    return pl.BlockSpec((None,) + tuple(shape), lambda *_: (layer,) + (0,) * nd, pipeline_mode=pl.Buffered(1))


def _mod_kernel(c_ref, w_ref, b_ref, o_ref):
    a = _silu(c_ref[...])
    o_ref[...] = _dot(a.astype(BF16), w_ref[...].astype(BF16)) + b_ref[...]


def _modulation(c, mod_w, mod_b, tn=1536):
    depth, d, d3 = mod_w.shape
    b = c.shape[0]
    return pl.pallas_call(
        _mod_kernel,
        grid=(depth, d3 // tn),
        in_specs=[
            pl.BlockSpec((b, d), lambda i, n: (0, 0)),
            pl.BlockSpec((None, d, tn), lambda i, n: (i, 0, n)),
            pl.BlockSpec((None, 1, tn), lambda i, n: (i, 0, n)),
        ],
        out_specs=pl.BlockSpec((None, b, tn), lambda i, n: (i, 0, n)),
        out_shape=jax.ShapeDtypeStruct((depth, b, d3), F32),
        compiler_params=_params("parallel", "parallel"),
        name="adaln_modulation",
    )(c, mod_w, mod_b.reshape(depth, 1, d3))


def _layer_vec_specs(layer, d):
    ng = pl.BlockSpec((None, 1, d), lambda b, m: (layer, 0, 0))

    def mod(which):
        return pl.BlockSpec((None, None, None, 1, d), lambda b, m: (layer, b, which, 0, 0))
    return ng, mod


def _rope_kernel(pos_ref, inv_ref, cos_ref, sin_ref):
    ang = pos_ref[...].astype(F32) * inv_ref[...]
    cos_ref[...] = jnp.cos(ang)
    sin_ref[...] = jnp.sin(ang)


def _rope_tables(positions, half, tm=2048):
    b, s = positions.shape
    inv = (ROPE_BASE ** (-jnp.arange(half, dtype=F32) / half)).reshape(1, half)
    tok = lambda w: pl.BlockSpec((None, tm, w), lambda bi, m: (bi, m, 0))
    return pl.pallas_call(
        _rope_kernel,
        grid=(b, s // tm),
        in_specs=[tok(1), pl.BlockSpec((1, half), lambda bi, m: (0, 0))],
        out_specs=[tok(half), tok(half)],
        out_shape=[jax.ShapeDtypeStruct((b, s, half), F32)] * 2,
        compiler_params=_params("parallel", "parallel"),
        name="rope_tables",
    )(positions.reshape(b, s, 1), inv)


def _ret_in_kernel(x_ref, ng_ref, sc_ref, sh_ref, cos_ref, sin_ref, w_ref,
                   q_ref, k_ref, v_ref, g_ref, *, heads, hk, dv, tn):
    h = _norm_mod(x_ref, ng_ref, sc_ref, sh_ref)
    cos = cos_ref[...]
    sin = sin_ref[...]
    half = hk // 2
    dk = heads * hk
    _project(h, w_ref, 2 * dk, dv, v_ref, tn)
    for col0, o_ref, scale in ((0, q_ref, hk ** -0.5), (dk, k_ref, None)):
        for hd in range(heads):
            r = _dot(h, w_ref[:, col0 + hd * hk:col0 + (hd + 1) * hk])
            x1 = r[:, :half]
            x2 = r[:, half:]
            o1 = x1 * cos - x2 * sin
            o2 = x1 * sin + x2 * cos
            if scale is not None:
                o1 = o1 * scale
                o2 = o2 * scale
            o_ref[:, hd * hk:hd * hk + half] = o1.astype(BF16)
            o_ref[:, hd * hk + half:(hd + 1) * hk] = o2.astype(BF16)
    _project(h, w_ref, 2 * dk + dv, dv, g_ref, tn)


def _ret_in_proj(x, cos, sin, norm_g, mod, layer, w_in, j, tm=512, tn=512):
    b, s, d = x.shape
    heads = RET_HEADS
    dk = d
    dv = 2 * d
    hk = dk // heads
    n_in = w_in.shape[-1]
    half = hk // 2
    ng, modspec = _layer_vec_specs(layer, d)
    tok = lambda w: pl.BlockSpec((None, tm, w), lambda bi, m: (bi, m, 0))
    return pl.pallas_call(
        functools.partial(_ret_in_kernel, heads=heads, hk=hk, dv=dv, tn=tn),
        grid=(b, s // tm),
        in_specs=[tok(d), ng, modspec(1), modspec(0), tok(half), tok(half), _resident((d, n_in), j)],
        out_specs=[tok(dk), tok(dk), tok(dv), tok(dv)],
        out_shape=[jax.ShapeDtypeStruct((b, s, w), BF16) for w in (dk, dk, dv, dv)],
        compiler_params=_params("parallel", "parallel"),
        name="retention_in_proj",
    )(x, norm_g, mod, mod, cos, sin, w_in)


def _residual_out(y_ref, w_ref, x_ref, gate_ref, fg_ref, o_ref, final):
    kdim = y_ref.shape[-1]
    half = kdim // 2
    r = _dot(y_ref[:, :half], w_ref[:half, :]) + _dot(y_ref[:, half:], w_ref[half:, :])
    xn = x_ref[...] + gate_ref[...] * r
    if final:
        ms = jnp.mean(xn * xn, axis=-1, keepdims=True)
        xn = xn * lax.rsqrt(ms + EPS) * fg_ref[...]
    o_ref[...] = xn


def _ret_core_kernel(q_ref, k_ref, v_ref, g_ref, x_ref, wf32_ref, gate_ref, fg_ref, o_ref,
                     state_ref, dmask_ref, qdec_ref, kdec_ref, y_ref, w_ref, *, heads, hk, hv, sup, final):
    log_gamma = [math.log1p(-(2.0 ** (-5.0 - h))) for h in range(heads)]
    _cast_once(wf32_ref, w_ref)

    @pl.when(pl.program_id(1) == 0)
    def _():
        state_ref[...] = jnp.zeros_like(state_ref)
        t = lax.broadcasted_iota(jnp.int32, (sup, sup), 0)
        s = lax.broadcasted_iota(jnp.int32, (sup, sup), 1)
        dist = jnp.abs(t - s).astype(F32)
        visible = (s // CHUNK) <= (t // CHUNK)
        pos = lax.broadcasted_iota(jnp.int32, (sup, hk), 0).astype(F32)
        for h in range(heads):
            dmask_ref[h] = jnp.where(visible, jnp.exp(log_gamma[h] * dist), 0.0)
            qdec_ref[h] = jnp.exp(log_gamma[h] * (pos + 1.0))
            kdec_ref[h] = jnp.exp(log_gamma[h] * (sup - 1.0 - pos))

    for r0 in range(0, q_ref.shape[0], sup):
        rows = slice(r0, r0 + sup)
        for h in range(heads):
            q = q_ref[rows, h * hk:(h + 1) * hk]
            k = k_ref[rows, h * hk:(h + 1) * hk]
            v = v_ref[rows, h * hv:(h + 1) * hv]
            scores = lax.dot_general(q, k, NT_DIMS, preferred_element_type=F32) * dmask_ref[h]
            state = state_ref[h]
            q_in = (q.astype(F32) * qdec_ref[h]).astype(BF16)
            o = _dot(scores.astype(BF16), v) + _dot(q_in, state.astype(BF16))
            k_in = (k.astype(F32) * kdec_ref[h]).astype(BF16)
            state_ref[h] = math.exp(log_gamma[h] * sup) * state + lax.dot_general(
                k_in, v, TN_DIMS, preferred_element_type=F32)
            mu = jnp.mean(o, axis=-1, keepdims=True)
            oc = o - mu
            var = jnp.mean(oc * oc, axis=-1, keepdims=True)
            gate = _silu(g_ref[rows, h * hv:(h + 1) * hv].astype(F32))
            y_ref[rows, h * hv:(h + 1) * hv] = (oc * lax.rsqrt(var + EPS) * gate).astype(BF16)
    _residual_out(y_ref, w_ref, x_ref, gate_ref, fg_ref, o_ref, final)


def _ret_core(q, k, v, g, x, w_out, j, mod, layer, final_g, final, tl=2 * RET_SUPER):
    b, s, dk = q.shape
    dv = v.shape[-1]
    d = x.shape[-1]
    heads = RET_HEADS
    hk, hv = dk // heads, dv // heads
    sup = RET_SUPER
    _, modspec = _layer_vec_specs(layer, d)
    tok = lambda w: pl.BlockSpec((None, tl, w), lambda bi, l: (bi, l, 0))
    return pl.pallas_call(
        functools.partial(_ret_core_kernel, heads=heads, hk=hk, hv=hv, sup=sup, final=final),
        grid=(b, s // tl),
        in_specs=[tok(dk), tok(dk), tok(dv), tok(dv), tok(d), _resident((dv, d), j), modspec(2), _resident((1, d))],
        out_specs=tok(d),
        out_shape=jax.ShapeDtypeStruct((b, s, d), F32),
        scratch_shapes=[pltpu.VMEM((heads, hk, hv), F32), pltpu.VMEM((heads, sup, sup), F32),
                        pltpu.VMEM((heads, sup, hk), F32), pltpu.VMEM((heads, sup, hk), F32),
                        pltpu.VMEM((tl, dv), BF16), pltpu.VMEM((dv, d), BF16)],
        compiler_params=_params("arbitrary", "arbitrary"),
        name="retention_core_out",
    )(q, k, v, g, x, w_out, mod, final_g)


def _fox_in_kernel(x_ref, ng_ref, sc_ref, sh_ref, wf32_ref, bf_ref,
                   q_ref, k_ref, v_ref, g_ref, cum_ref, first_ref, last_ref, carry_ref, w_ref, *, width, tn):
    _cast_once(wf32_ref, w_ref)

    @pl.when(pl.program_id(1) == 0)
    def _():
        carry_ref[...] = jnp.zeros_like(carry_ref)

    h = _norm_mod(x_ref, ng_ref, sc_ref, sh_ref)
    log_f = _log_sigmoid(_dot(h, w_ref[:, 4 * width:]) + bf_ref[...])
    _project(h, w_ref, 0, width, q_ref, tn)
    rows = log_f.shape[0]
    tri = jnp.where(_lower_tri(rows), 1.0, 0.0).astype(BF16)
    cum = _tri_cumsum(tri, log_f) + carry_ref[...]
    carry_ref[...] = cum[rows - 1:rows, :]
    cum_ref[...] = cum
    first_ref[...] = cum[0:1, :]
    last_ref[0:1, :] = cum[rows // 2 - 1:rows // 2, :]
    last_ref[1:2, :] = cum[rows - 1:rows, :]
    for idx, o_ref in ((1, k_ref), (2, v_ref), (3, g_ref)):
        _project(h, w_ref, idx * width, width, o_ref, tn)


def _fox_in_proj(x, norm_g, mod, layer, w_in, j, b_f, tm=512, tn=512):
    b, s, d = x.shape
    heads = FOX_HEADS
    n_in = w_in.shape[-1]
    width = (n_in - heads) // 4
    bias = jnp.pad(b_f.astype(F32), (0, LANES - heads)).reshape(1, LANES)
    ng, modspec = _layer_vec_specs(layer, d)
    tok = lambda w: pl.BlockSpec((None, tm, w), lambda bi, m: (bi, m, 0))
    edge = lambda n: pl.BlockSpec((None, None, n, LANES), lambda bi, m: (bi, m, 0, 0))
    return pl.pallas_call(
        functools.partial(_fox_in_kernel, width=width, tn=tn),
        grid=(b, s // tm),
        in_specs=[tok(d), ng, modspec(1), modspec(0), _resident((d, n_in), j), _resident((1, LANES))],
        out_specs=[tok(width)] * 4 + [tok(LANES), edge(1), edge(2)],
        out_shape=[jax.ShapeDtypeStruct((b, s, width), BF16)] * 4 + [jax.ShapeDtypeStruct((b, s, LANES), F32)]
                  + [jax.ShapeDtypeStruct((b, s // tm, n, LANES), F32) for n in (1, 2)],
        scratch_shapes=[pltpu.VMEM((1, LANES), F32), pltpu.VMEM((d, 4 * width + LANES), BF16)],
        compiler_params=_params("arbitrary", "arbitrary"),
        name="fox_in_proj",
    )(x, norm_g, mod, mod, w_in, bias)


def _fox_flash_kernel(*refs, bq, hd, seq, bounded):
    if bounded:
        (start_ref, q_ref, k_ref, v_ref, g_ref, cum_ref, qg_ref, kg_ref, bound_ref, y_ref,
         kx_ref, vx_ref, qx_ref, acc_ref) = refs
    else:
        (q_ref, k_ref, v_ref, g_ref, cum_ref, qg_ref, kg_ref, y_ref,
         kx_ref, vx_ref, qx_ref, acc_ref, m_ref) = refs
    pair = pl.program_id(1)
    lane = lax.broadcasted_iota(jnp.int32, (1, LANES), 1)
    first = lane < hd
    sels = (first, jnp.logical_not(first))
    aux0 = (hd, 0)

    def head_norm(x, gain):
        sq = x * x
        s0 = jnp.sum(jnp.where(first, sq, 0.0), axis=-1, keepdims=True)
        s1 = jnp.sum(jnp.where(first, 0.0, sq), axis=-1, keepdims=True)
        ms = jnp.where(first, s0, s1) * (1.0 / hd)
        return x * lax.rsqrt(ms + EPS) * gain

    def lane_move(offset):
        src = lax.broadcasted_iota(jnp.int32, (LANES, LANES), 0)
        dst = lax.broadcasted_iota(jnp.int32, (LANES, LANES), 1)
        blocks = []
        for t in range(3):
            hit = jnp.logical_or(jnp.logical_and(src == 2 * pair, dst == aux0[0] + offset + t),
                                 jnp.logical_and(src == 2 * pair + 1, dst == aux0[1] + offset + t))
            blocks.append(jnp.where(hit, 1.0, 0.0))
        return jnp.concatenate(blocks, axis=0).astype(BF16)

    def bias_lanes(cum, offset, select):
        moved = _dot(jnp.concatenate(_split3(cum), axis=1), select)
        out = []
        for j in range(2):
            one0 = aux0[j] + 3 - offset
            ones = jnp.where(jnp.logical_and(lane >= one0, lane < one0 + 3), 1.0, 0.0)
            out.append(moved + ones)
        return out

    one = jnp.ones((), BF16)
    move_k = lane_move(0)
    move_q = lane_move(3)
    for r0 in range(0, seq, bq):
        kn = head_norm(k_ref[r0:r0 + bq, :].astype(F32), kg_ref[...])
        bias = bias_lanes(cum_ref[r0:r0 + bq, :] * (-LOG2E), 0, move_k)
        v = v_ref[r0:r0 + bq, :]
        for j in range(2):
            kx_ref[j, r0:r0 + bq, :] = jnp.where(sels[j], kn, bias[j]).astype(BF16)
            vx_ref[j, r0:r0 + bq, :] = jnp.where(sels[j], v, one)

    def query_block(qi, carry):
        q0 = pl.multiple_of(qi * bq, bq)
        qrows = pl.ds(q0, bq)
        qn = head_norm(q_ref[qrows, :].astype(F32), qg_ref[...]) * (hd ** -0.5 * LOG2E)
        cum_q = cum_ref[qrows, :] * LOG2E
        if bounded:
            cum_q = cum_q - bound_ref[...]
        bias = bias_lanes(cum_q, 3, move_q)
        for j in range(2):
            qx_ref[j] = jnp.where(sels[j], qn, bias[j]).astype(BF16)
        acc_ref[...] = jnp.zeros_like(acc_ref)
        if not bounded:
            m_ref[...] = jnp.full_like(m_ref, NEG_BIG)
        causal_half = _lower_tri(bq // 2)

        def bounded_blocks(full, diagonal, lead_half=None):
            half = bq // 2
            work = []
            if lead_half is not None:
                rk = pl.ds(pl.multiple_of(lead_half * half, half), half)
                work += [(j, slice(0, bq), rk, None) for j in range(2)]
            for kb in full:
                r0 = pl.multiple_of(kb * bq, bq)
                work += [(j, slice(0, bq), pl.ds(r0, bq), None) for j in range(2)]
            if diagonal:
                for j in range(2):
                    work.append((j, slice(0, half), pl.ds(q0, half), "all"))
                    work.append((j, slice(half, bq), qrows, "right"))
            logits = [lax.dot_general(qx_ref[j, rq, :], kx_ref[j, rk, :], NT_DIMS, preferred_element_type=F32)
                      for j, rq, rk, _ in work]
            add = {}
            for (j, rq, rk, masked), s in zip(work, logits):
                if masked == "all":
                    s = jnp.where(causal_half, s, NEG_BIG)
                elif masked == "right":
                    s = jnp.concatenate([s[:, :half], jnp.where(causal_half, s[:, half:], NEG_BIG)], axis=1)
                pv = _dot(jnp.exp2(s).astype(BF16), vx_ref[j, rk, :])
                key = (j, rq.start, rq.stop)
                add[key] = pv if key not in add else add[key] + pv
            for (j, r0, r1), pv in add.items():
                acc_ref[j, r0:r1, :] += pv

        def step(kb, masked):
            rk = pl.ds(pl.multiple_of(kb * bq, bq), bq)
            logits = [lax.dot_general(qx_ref[j], kx_ref[j, rk, :], NT_DIMS, preferred_element_type=F32)
                      for j in range(2)]
            for j in range(2):
                s = jnp.where(_lower_tri(bq), logits[j], NEG_BIG) if masked else logits[j]
                m_prev = m_ref[j]
                m_new = jnp.maximum(m_prev, jnp.max(s, axis=-1, keepdims=True))
                p = jnp.exp2(s - jnp.concatenate([m_new] * (bq // LANES), axis=1))
                acc_ref[j] = jnp.exp2(m_prev - m_new) * acc_ref[j] + _dot(p.astype(BF16), vx_ref[j, rk, :])
                m_ref[j] = m_new

        if bounded:
            first_half = start_ref[(pl.program_id(0) * pl.num_programs(1) + pair) * (seq // bq) + qi]
            lead = jnp.bitwise_and(first_half, 1) == 1
            start = jnp.right_shift(first_half + 1, 1)
            count = qi - start

            def body(t, carry):
                bounded_blocks([start + 2 * t, start + 2 * t + 1], False)
                return carry

            lax.fori_loop(0, jnp.right_shift(count, 1), body, 0)
            odd = jnp.bitwise_and(count, 1) == 1
            for is_odd in (True, False):
                for has_lead in (True, False):
                    @pl.when(jnp.logical_and(odd == is_odd, lead == has_lead))
                    def _():
                        bounded_blocks([qi - 1] if is_odd else [], True, first_half if has_lead else None)
        else:
            def body(kb, carry):
                step(kb, False)
                return carry

            lax.fori_loop(0, qi, body, 0)
            step(qi, True)
        num = jnp.where(first, acc_ref[0], acc_ref[1])
        den = jnp.where(first, pltpu.roll(acc_ref[0], hd, axis=1), pltpu.roll(acc_ref[1], hd, axis=1))
        y_ref[qrows, :] = (num / den * _silu(g_ref[qrows, :].astype(F32))).astype(BF16)
        return carry

    lax.fori_loop(0, seq // bq, query_block, 0)


def _fox_flash(q, k, v, g, cum, cum_first, cum_last, q_gain, k_gain, bq=512):
    b, s, width = q.shape
    assert cum_first.shape[1] == s // bq and cum_last.shape[2] == 2
    heads = FOX_HEADS
    hd = width // heads
    assert 2 * hd == LANES
    pairs = heads // 2
    qg = jnp.tile(q_gain.astype(F32), 2).reshape(1, LANES)
    kg = jnp.tile(k_gain.astype(F32), 2).reshape(1, LANES)
    nq = s // bq
    full = pl.BlockSpec((None, s, LANES), lambda bi, p, *_: (bi, 0, p))
    row = pl.BlockSpec((1, LANES), lambda bi, p, *_: (0, 0))

    def call(bounded, *extra):
        grid_spec = pltpu.PrefetchScalarGridSpec(
            num_scalar_prefetch=1 if bounded else 0,
            grid=(b, pairs),
            in_specs=[full, full, full, full, pl.BlockSpec((None, s, LANES), lambda bi, p, *_: (bi, 0, 0)),
                      row, row] + ([row] if bounded else []),
            out_specs=full,
            scratch_shapes=[pltpu.VMEM((2, s, LANES), BF16), pltpu.VMEM((2, s, LANES), BF16),
                            pltpu.VMEM((2, bq, LANES), BF16), pltpu.VMEM((2, bq, LANES), F32)]
                           + ([] if bounded else [pltpu.VMEM((2, bq, LANES), F32)]))
        return pl.pallas_call(
            functools.partial(_fox_flash_kernel, bq=bq, hd=hd, seq=s, bounded=bounded),
            grid_spec=grid_spec,
            out_shape=jax.ShapeDtypeStruct((b, s, width), BF16),
            compiler_params=_params("parallel", "parallel"),
            name="fox_flash_bounded" if bounded else "fox_flash_online",
        )(*extra[:1], q, k, v, g, cum, qg, kg, *extra[1:])

    def first_key_blocks():
        at_first = cum_first[:, :, 0, :heads]
        at_last = cum_last[:, :, :, :heads].reshape(b, 2 * nq, heads)
        gap = at_first[:, :, None, :] - at_last[:, None, :, :]
        blocks = jnp.arange(nq, dtype=jnp.int32)
        halves = jnp.arange(2 * nq, dtype=jnp.int32)
        earlier = (halves[None, :] < 2 * blocks[:, None])[None, :, :, None]
        needed = jnp.logical_and(gap * LOG2E > -FOX_SKIP_BITS, earlier)
        first = jnp.where(jnp.any(needed, axis=2), jnp.argmax(needed, axis=2).astype(jnp.int32),
                          2 * blocks[None, :, None])
        first = jnp.min(first.reshape(b, nq, pairs, 2), axis=-1)
        return jnp.transpose(first, (0, 2, 1)).reshape(-1)

    bound = FOX_BOUND_MARGIN * LOG2E * hd ** 0.5 * jnp.max(jnp.abs(qg)) * jnp.max(jnp.abs(kg))
    return lax.cond(bound <= FOX_BOUND_MAX,
                    lambda: call(True, first_key_blocks(), jnp.full((1, LANES), bound, F32)),
                    lambda: call(False))


def _gla_in_kernel(x_ref, ng_ref, sc_ref, sh_ref, wf32_ref, wg2f32_ref, bg_ref,
                   q_ref, k_ref, v_ref, g_ref, la_ref, w_ref, wg2_ref, *, dk, dv, tn):
    _cast_once(wf32_ref, w_ref)

    @pl.when(jnp.logical_and(pl.program_id(0) == 0, pl.program_id(1) == 0))
    def _():
        wg2_ref[...] = jnp.zeros_like(wg2_ref)
        wg2_ref[:wg2f32_ref.shape[0], :] = wg2f32_ref[...].astype(BF16)

    h = _norm_mod(x_ref, ng_ref, sc_ref, sh_ref)
    main = 2 * dk + 2 * dv
    r = _dot(h, w_ref[:, main:])
    _project(h, w_ref, 0, dk, q_ref, tn)
    z = _dot(r.astype(BF16), wg2_ref[...]) + bg_ref[...]
    _project(h, w_ref, dk, dk, k_ref, tn)
    la_ref[...] = _log_sigmoid(z) * (1.0 / GLA_TAU)
    _project(h, w_ref, 2 * dk, dv, v_ref, tn)
    _project(h, w_ref, 2 * dk + dv, dv, g_ref, tn)


def _gla_in_proj(x, norm_g, mod, layer, w_in, w_gate2, j, b_gate, tm=512, tn=512):
    b, s, d = x.shape
    dk, dv, rank = d // 2, d, GLA_RANK
    main = 2 * dk + 2 * dv
    ng, modspec = _layer_vec_specs(layer, d)
    tok = lambda w, : pl.BlockSpec((None, tm, w), lambda bi, m: (bi, m, 0))
    return pl.pallas_call(
        functools.partial(_gla_in_kernel, dk=dk, dv=dv, tn=tn),
        grid=(b, s // tm),
        in_specs=[tok(d), ng, modspec(1), modspec(0), _resident((d, main + rank), j),
                  _resident((rank, dk), j), _resident((1, dk))],
        out_specs=[tok(dk), tok(dk), tok(dv), tok(dv), tok(dk)],
        out_shape=[jax.ShapeDtypeStruct((b, s, w), BF16) for w in (dk, dk, dv, dv)]
                  + [jax.ShapeDtypeStruct((b, s, dk), F32)],
        scratch_shapes=[pltpu.VMEM((d, main + LANES), BF16), pltpu.VMEM((LANES, dk), BF16)],
        compiler_params=_params("arbitrary", "arbitrary"),
        name="gla_in_proj",
    )(x, norm_g, mod, mod, w_in, w_gate2, b_gate.astype(F32).reshape(1, dk))


def _gla_core_kernel(q_ref, k_ref, v_ref, g_ref, la_ref, x_ref, wf32_ref, gate_ref, fg_ref, o_ref,
                     state_ref, y_ref, w_ref, *, heads, hk, hv, nchunk, final):
    _cast_once(wf32_ref, w_ref)

    @pl.when(pl.program_id(1) == 0)
    def _():
        state_ref[...] = jnp.zeros_like(state_ref)

    tile = nchunk * CHUNK
    row = lax.broadcasted_iota(jnp.int32, (tile, tile), 0)
    col = lax.broadcasted_iota(jnp.int32, (tile, tile), 1)
    same = (row // CHUNK) == (col // CHUNK)
    causal = row >= col
    tri = jnp.where(jnp.logical_and(same, causal), 1.0, 0.0).astype(BF16)
    scale = hk ** -0.5
    dk = heads * hk
    chunk_rows = [slice(c * CHUNK, (c + 1) * CHUNK) for c in range(nchunk)]

    def run_tile(rows):
        cb = _tri_cumsum(tri, la_ref[rows, :], pieces=2)
        lasts = [cb[r.stop - 1:r.stop, :] for r in chunk_rows]
        cb_last = jnp.concatenate([jnp.broadcast_to(last, (CHUNK, dk)) for last in lasts], axis=0)
        eb = jnp.exp(cb)
        enb = jnp.exp(-cb)
        qf = q_ref[rows, :].astype(F32) * scale
        kf = k_ref[rows, :].astype(F32)
        q_e = (qf * eb).astype(BF16)
        q_n = (qf * enb).astype(BF16)
        k_n = (kf * enb).astype(BF16)
        k_e = (kf * eb).astype(BF16)
        k_in = (kf * jnp.exp(cb_last - cb)).astype(BF16)
        for h in range(heads):
            ks = slice(h * hk, (h + 1) * hk)
            vs = slice(h * hv, (h + 1) * hv)
            v = v_ref[rows, vs]
            a_causal = lax.dot_general(q_e[:, ks], k_n[:, ks], NT_DIMS, preferred_element_type=F32)
            a_anti = lax.dot_general(q_n[:, ks], k_e[:, ks], NT_DIMS, preferred_element_type=F32)
            attn = jnp.where(same, jnp.where(causal, a_causal, a_anti), 0.0).astype(BF16)
            updates = [lax.dot_general(k_in[r, ks], v[r, :], TN_DIMS, preferred_element_type=F32)
                       for r in chunk_rows]
            o = _dot(attn, v)
            state = state_ref[h]
            carried = []
            for c in range(nchunk):
                carried.append(state.astype(BF16))
                decay_col = jnp.broadcast_to(jnp.exp(lasts[c][:, ks]), (hk, hk)).T
                state = state * jnp.concatenate([decay_col] * (hv // hk), axis=1) + updates[c]
            state_ref[h] = state
            o = o + jnp.concatenate([_dot(q_e[r, ks], carried[c]) for c, r in enumerate(chunk_rows)], axis=0)
            ms = jnp.mean(o * o, axis=-1, keepdims=True)
            gate = _silu(g_ref[rows, vs].astype(F32))
            y_ref[rows, vs] = (o * lax.rsqrt(ms + EPS) * gate).astype(BF16)

    for t0 in range(0, q_ref.shape[0], tile):
        run_tile(slice(t0, t0 + tile))
    _residual_out(y_ref, w_ref, x_ref, gate_ref, fg_ref, o_ref, final)


def _gla_core(q, k, v, g, la, x, w_out, j, mod, layer, final_g, final, tile=256, tl=1024):
    b, s, dk = q.shape
    dv = v.shape[-1]
    d = x.shape[-1]
    heads = GLA_HEADS
    hk, hv = dk // heads, dv // heads
    _, modspec = _layer_vec_specs(layer, d)
    tok = lambda w: pl.BlockSpec((None, tl, w), lambda bi, l: (bi, l, 0))
    return pl.pallas_call(
        functools.partial(_gla_core_kernel, heads=heads, hk=hk, hv=hv, nchunk=tile // CHUNK, final=final),
        grid=(b, s // tl),
        in_specs=[tok(dk), tok(dk), tok(dv), tok(dv), tok(dk), tok(d), _resident((dv, d), j), modspec(2),
                  _resident((1, d))],
        out_specs=tok(d),
        out_shape=jax.ShapeDtypeStruct((b, s, d), F32),
        scratch_shapes=[pltpu.VMEM((heads, hk, hv), F32), pltpu.VMEM((tl, dv), BF16), pltpu.VMEM((dv, d), BF16)],
        compiler_params=_params("arbitrary", "arbitrary"),
        name="gla_core_out",
    )(q, k, v, g, la, x, w_out, mod, final_g)


def _out_kernel(y_ref, wf32_ref, x_ref, gate_ref, fg_ref, o_ref, w_ref, *, final):
    _cast_once(wf32_ref, w_ref)
    _residual_out(y_ref, w_ref, x_ref, gate_ref, fg_ref, o_ref, final)


def _out_proj(y, w_out, j, x, mod, layer, final_g, final, tm=1024):
    b, s, d = x.shape
    kdim = y.shape[-1]
    _, modspec = _layer_vec_specs(layer, d)
    tok = lambda w: pl.BlockSpec((None, tm, w), lambda bi, m: (bi, m, 0))
    return pl.pallas_call(
        functools.partial(_out_kernel, final=final),
        grid=(b, s // tm),
        in_specs=[tok(kdim), _resident((kdim, d), j), tok(d), modspec(2), _resident((1, d))],
        out_specs=tok(d),
        out_shape=jax.ShapeDtypeStruct((b, s, d), F32),
        scratch_shapes=[pltpu.VMEM((kdim, d), BF16)],
        compiler_params=_params("arbitrary", "arbitrary"),
        name="out_proj_residual",
    )(y, w_out, x, mod, final_g)


def kernel(x, c, positions, mod_w, mod_b, norm_g, ret_w_in, ret_w_out, fox_w_in, fox_b_f, fox_q_gain,
           fox_k_gain, fox_w_out, gla_w_in, gla_w_gate2, gla_b_gate, gla_w_out, final_g):
    depth, d, _ = mod_w.shape
    b = x.shape[0]
    mod = _modulation(c, mod_w, mod_b).reshape(depth, b, 3, 1, d)
    ng = norm_g.astype(F32).reshape(depth, 1, d)
    fg = final_g.astype(F32).reshape(1, d)
    cos, sin = _rope_tables(positions, d // RET_HEADS // 2)
    ret_w_in_bf = ret_w_in.astype(BF16)
    for i in range(depth):
        j = i // N_MIXERS
        kind = i % N_MIXERS
        final = i == depth - 1
        if kind == 0:
            q, k, v, g = _ret_in_proj(x, cos, sin, ng, mod, i, ret_w_in_bf, j)
            x = _ret_core(q, k, v, g, x, ret_w_out, j, mod, i, fg, final)
        elif kind == 1:
            q, k, v, g, cum, cum_first, cum_last = _fox_in_proj(x, ng, mod, i, fox_w_in, j, fox_b_f[j])
            y = _fox_flash(q, k, v, g, cum, cum_first, cum_last, fox_q_gain[j], fox_k_gain[j])
            x = _out_proj(y, fox_w_out, j, x, mod, i, fg, final)
        else:
            q, k, v, g, la = _gla_in_proj(x, ng, mod, i, gla_w_in, gla_w_gate2, j, gla_b_gate[j])
            x = _gla_core(q, k, v, g, la, x, gla_w_out, j, mod, i, fg, final)
    return x
```

```python
import functools
import math

import jax
import jax.numpy as jnp
from jax import lax
from jax.experimental import pallas as pl
from jax.experimental.pallas import tpu as pltpu

F32 = jnp.float32
BF16 = jnp.bfloat16

EPS = 1e-6
N_MIXERS = 3
CHUNK = 64
RET_HEADS = 4
ROPE_BASE = 10000.0
FOX_HEADS = 16
GLA_HEADS = 4
GLA_RANK = 16
GLA_TAU = 16.0

LANES = 128
RET_SUPER = 256
NEG_BIG = -1e30
LOG2E = math.log2(math.e)
FOX_BOUND_MARGIN = 1.02
FOX_BOUND_MAX = 48.0
FOX_SKIP_BITS = 160.0
VMEM_LIMIT = 56 * 1024 * 1024

NT_DIMS = (((1,), (1,)), ((), ()))
TN_DIMS = (((0,), (0,)), ((), ()))


def _silu(x):
    return x * (1.0 / (1.0 + jnp.exp(-x)))


def _log_sigmoid(z):
    return jnp.minimum(z, 0.0) - jnp.log1p(jnp.exp(-jnp.abs(z)))


def _dot(a, b):
    return jnp.dot(a, b, preferred_element_type=F32)


def _split3(x):
    hi = x.astype(BF16)
    r1 = x - hi.astype(F32)
    mid = r1.astype(BF16)
    lo = (r1 - mid.astype(F32)).astype(BF16)
    return hi, mid, lo


def _tri_cumsum(tri, x, pieces=3):
    parts = _split3(x)[:pieces]
    out = _dot(tri, parts[0])
    for part in parts[1:]:
        out = out + _dot(tri, part)
    return out


def _lower_tri(n):
    row = lax.broadcasted_iota(jnp.int32, (n, n), 0)
    col = lax.broadcasted_iota(jnp.int32, (n, n), 1)
    return row >= col


def _norm_mod(x_ref, ng_ref, sc_ref, sh_ref):
    x = x_ref[...]
    ms = jnp.mean(x * x, axis=-1, keepdims=True)
    gain = ng_ref[...] * (1.0 + sc_ref[...])
    return (x * lax.rsqrt(ms + EPS) * gain + sh_ref[...]).astype(BF16)


def _project(h, w_ref, col0, width, o_ref, tn):
    for n0 in range(0, width, tn):
        o_ref[:, n0:n0 + tn] = _dot(h, w_ref[:, col0 + n0:col0 + n0 + tn]).astype(o_ref.dtype)


def _cast_once(w_ref, wbf_ref, tn=512):
    @pl.when(jnp.logical_and(pl.program_id(0) == 0, pl.program_id(1) == 0))
    def _():
        cols = w_ref.shape[1]
        aligned = cols // LANES * LANES
        for n0 in range(0, aligned, tn):
            n1 = min(n0 + tn, aligned)
            wbf_ref[:, n0:n1] = w_ref[:, n0:n1].astype(BF16)
        if aligned < cols:
            wbf_ref[:, aligned:] = jnp.zeros((w_ref.shape[0], wbf_ref.shape[1] - aligned), BF16)
            wbf_ref[:, aligned:cols] = w_ref[:, aligned:cols].astype(BF16)


def _params(*sem):
    return pltpu.CompilerParams(dimension_semantics=sem, vmem_limit_bytes=VMEM_LIMIT)


def _resident(shape, layer=None):
    nd = len(shape)
    if layer is None:
        return pl.BlockSpec(shape, lambda *_: (0,) * nd, pipeline_mode=pl.Buffered(1))
    return pl.BlockSpec((None,) + tuple(shape), lambda *_: (layer,) + (0,) * nd, pipeline_mode=pl.Buffered(1))


def _mod_kernel(c_ref, w_ref, b_ref, o_ref):
    a = _silu(c_ref[...])
    o_ref[...] = _dot(a.astype(BF16), w_ref[...].astype(BF16)) + b_ref[...]


def _modulation(c, mod_w, mod_b, tn=1536):
    depth, d, d3 = mod_w.shape
    b = c.shape[0]
    return pl.pallas_call(
        _mod_kernel,
        grid=(depth, d3 // tn),
        in_specs=[
            pl.BlockSpec((b, d), lambda i, n: (0, 0)),
            pl.BlockSpec((None, d, tn), lambda i, n: (i, 0, n)),
            pl.BlockSpec((None, 1, tn), lambda i, n: (i, 0, n)),
        ],
        out_specs=pl.BlockSpec((None, b, tn), lambda i, n: (i, 0, n)),
        out_shape=jax.ShapeDtypeStruct((depth, b, d3), F32),
        compiler_params=_params("parallel", "parallel"),
        name="adaln_modulation",
    )(c, mod_w, mod_b.reshape(depth, 1, d3))


def _layer_vec_specs(layer, d):
    ng = pl.BlockSpec((None, 1, d), lambda b, m: (layer, 0, 0))

    def mod(which):
        return pl.BlockSpec((None, None, None, 1, d), lambda b, m: (layer, b, which, 0, 0))
    return ng, mod


def _rope_kernel(pos_ref, inv_ref, cos_ref, sin_ref):
    ang = pos_ref[...].astype(F32) * inv_ref[...]
    cos_ref[...] = jnp.cos(ang)
    sin_ref[...] = jnp.sin(ang)


def _rope_tables(positions, half, tm=2048):
    b, s = positions.shape
    inv = (ROPE_BASE ** (-jnp.arange(half, dtype=F32) / half)).reshape(1, half)
    tok = lambda w: pl.BlockSpec((None, tm, w), lambda bi, m: (bi, m, 0))
    return pl.pallas_call(
        _rope_kernel,
        grid=(b, s // tm),
        in_specs=[tok(1), pl.BlockSpec((1, half), lambda bi, m: (0, 0))],
        out_specs=[tok(half), tok(half)],
        out_shape=[jax.ShapeDtypeStruct((b, s, half), F32)] * 2,
        compiler_params=_params("parallel", "parallel"),
        name="rope_tables",
    )(positions.reshape(b, s, 1), inv)


def _ret_in_kernel(x_ref, ng_ref, sc_ref, sh_ref, cos_ref, sin_ref, w_ref,
                   q_ref, k_ref, v_ref, g_ref, *, heads, hk, dv, tn):
    h = _norm_mod(x_ref, ng_ref, sc_ref, sh_ref)
    cos = cos_ref[...]
    sin = sin_ref[...]
    half = hk // 2
    dk = heads * hk
    _project(h, w_ref, 2 * dk, dv, v_ref, tn)
    for col0, o_ref, scale in ((0, q_ref, hk ** -0.5), (dk, k_ref, None)):
        for hd in range(heads):
            r = _dot(h, w_ref[:, col0 + hd * hk:col0 + (hd + 1) * hk])
            x1 = r[:, :half]
            x2 = r[:, half:]
            o1 = x1 * cos - x2 * sin
            o2 = x1 * sin + x2 * cos
            if scale is not None:
                o1 = o1 * scale
                o2 = o2 * scale
            o_ref[:, hd * hk:hd * hk + half] = o1.astype(BF16)
            o_ref[:, hd * hk + half:(hd + 1) * hk] = o2.astype(BF16)
    _project(h, w_ref, 2 * dk + dv, dv, g_ref, tn)


def _ret_in_proj(x, cos, sin, norm_g, mod, layer, w_in, j, tm=512, tn=512):
    b, s, d = x.shape
    heads = RET_HEADS
    dk = d
    dv = 2 * d
    hk = dk // heads
    n_in = w_in.shape[-1]
    half = hk // 2
    ng, modspec = _layer_vec_specs(layer, d)
    tok = lambda w: pl.BlockSpec((None, tm, w), lambda bi, m: (bi, m, 0))
    return pl.pallas_call(
        functools.partial(_ret_in_kernel, heads=heads, hk=hk, dv=dv, tn=tn),
        grid=(b, s // tm),
        in_specs=[tok(d), ng, modspec(1), modspec(0), tok(half), tok(half), _resident((d, n_in), j)],
        out_specs=[tok(dk), tok(dk), tok(dv), tok(dv)],
        out_shape=[jax.ShapeDtypeStruct((b, s, w), BF16) for w in (dk, dk, dv, dv)],
        compiler_params=_params("parallel", "parallel"),
        name="retention_in_proj",
    )(x, norm_g, mod, mod, cos, sin, w_in)


def _residual_out(y_ref, w_ref, x_ref, gate_ref, fg_ref, o_ref, final):
    kdim = y_ref.shape[-1]
    half = kdim // 2
    r = _dot(y_ref[:, :half], w_ref[:half, :]) + _dot(y_ref[:, half:], w_ref[half:, :])
    xn = x_ref[...] + gate_ref[...] * r
    if final:
        ms = jnp.mean(xn * xn, axis=-1, keepdims=True)
        xn = xn * lax.rsqrt(ms + EPS) * fg_ref[...]
    o_ref[...] = xn


def _ret_core_kernel(q_ref, k_ref, v_ref, g_ref, x_ref, wf32_ref, gate_ref, fg_ref, o_ref,
                     state_ref, dmask_ref, qdec_ref, kdec_ref, y_ref, w_ref, *, heads, hk, hv, sup, final):
    log_gamma = [math.log1p(-(2.0 ** (-5.0 - h))) for h in range(heads)]
    _cast_once(wf32_ref, w_ref)

    @pl.when(pl.program_id(1) == 0)
    def _():
        state_ref[...] = jnp.zeros_like(state_ref)
        t = lax.broadcasted_iota(jnp.int32, (sup, sup), 0)
        s = lax.broadcasted_iota(jnp.int32, (sup, sup), 1)
        dist = jnp.abs(t - s).astype(F32)
        visible = (s // CHUNK) <= (t // CHUNK)
        pos = lax.broadcasted_iota(jnp.int32, (sup, hk), 0).astype(F32)
        for h in range(heads):
            dmask_ref[h] = jnp.where(visible, jnp.exp(log_gamma[h] * dist), 0.0)
            qdec_ref[h] = jnp.exp(log_gamma[h] * (pos + 1.0))
            kdec_ref[h] = jnp.exp(log_gamma[h] * (sup - 1.0 - pos))

    for r0 in range(0, q_ref.shape[0], sup):
        rows = slice(r0, r0 + sup)
        for h in range(heads):
            q = q_ref[rows, h * hk:(h + 1) * hk]
            k = k_ref[rows, h * hk:(h + 1) * hk]
            v = v_ref[rows, h * hv:(h + 1) * hv]
            scores = lax.dot_general(q, k, NT_DIMS, preferred_element_type=F32) * dmask_ref[h]
            state = state_ref[h]
            q_in = (q.astype(F32) * qdec_ref[h]).astype(BF16)
            o = _dot(scores.astype(BF16), v) + _dot(q_in, state.astype(BF16))
            k_in = (k.astype(F32) * kdec_ref[h]).astype(BF16)
            state_ref[h] = math.exp(log_gamma[h] * sup) * state + lax.dot_general(
                k_in, v, TN_DIMS, preferred_element_type=F32)
            mu = jnp.mean(o, axis=-1, keepdims=True)
            oc = o - mu
            var = jnp.mean(oc * oc, axis=-1, keepdims=True)
            gate = _silu(g_ref[rows, h * hv:(h + 1) * hv].astype(F32))
            y_ref[rows, h * hv:(h + 1) * hv] = (oc * lax.rsqrt(var + EPS) * gate).astype(BF16)
    _residual_out(y_ref, w_ref, x_ref, gate_ref, fg_ref, o_ref, final)


def _ret_core(q, k, v, g, x, w_out, j, mod, layer, final_g, final, tl=2 * RET_SUPER):
    b, s, dk = q.shape
    dv = v.shape[-1]
    d = x.shape[-1]
    heads = RET_HEADS
    hk, hv = dk // heads, dv // heads
    sup = RET_SUPER
    _, modspec = _layer_vec_specs(layer, d)
    tok = lambda w: pl.BlockSpec((None, tl, w), lambda bi, l: (bi, l, 0))
    return pl.pallas_call(
        functools.partial(_ret_core_kernel, heads=heads, hk=hk, hv=hv, sup=sup, final=final),
        grid=(b, s // tl),
        in_specs=[tok(dk), tok(dk), tok(dv), tok(dv), tok(d), _resident((dv, d), j), modspec(2), _resident((1, d))],
        out_specs=tok(d),
        out_shape=jax.ShapeDtypeStruct((b, s, d), F32),
        scratch_shapes=[pltpu.VMEM((heads, hk, hv), F32), pltpu.VMEM((heads, sup, sup), F32),
                        pltpu.VMEM((heads, sup, hk), F32), pltpu.VMEM((heads, sup, hk), F32),
                        pltpu.VMEM((tl, dv), BF16), pltpu.VMEM((dv, d), BF16)],
        compiler_params=_params("arbitrary", "arbitrary"),
        name="retention_core_out",
    )(q, k, v, g, x, w_out, mod, final_g)


def _fox_in_kernel(x_ref, ng_ref, sc_ref, sh_ref, wf32_ref, bf_ref,
                   q_ref, k_ref, v_ref, g_ref, cum_ref, first_ref, last_ref, carry_ref, w_ref, *, width, tn):
    _cast_once(wf32_ref, w_ref)

    @pl.when(pl.program_id(1) == 0)
    def _():
        carry_ref[...] = jnp.zeros_like(carry_ref)

    h = _norm_mod(x_ref, ng_ref, sc_ref, sh_ref)
    log_f = _log_sigmoid(_dot(h, w_ref[:, 4 * width:]) + bf_ref[...])
    _project(h, w_ref, 0, width, q_ref, tn)
    rows = log_f.shape[0]
    tri = jnp.where(_lower_tri(rows), 1.0, 0.0).astype(BF16)
    cum = _tri_cumsum(tri, log_f) + carry_ref[...]
    carry_ref[...] = cum[rows - 1:rows, :]
    cum_ref[...] = cum
    first_ref[...] = cum[0:1, :]
    last_ref[0:1, :] = cum[rows // 2 - 1:rows // 2, :]
    last_ref[1:2, :] = cum[rows - 1:rows, :]
    for idx, o_ref in ((1, k_ref), (2, v_ref), (3, g_ref)):
        _project(h, w_ref, idx * width, width, o_ref, tn)


def _fox_in_proj(x, norm_g, mod, layer, w_in, j, b_f, tm=512, tn=512):
    b, s, d = x.shape
    heads = FOX_HEADS
    n_in = w_in.shape[-1]
    width = (n_in - heads) // 4
    bias = jnp.pad(b_f.astype(F32), (0, LANES - heads)).reshape(1, LANES)
    ng, modspec = _layer_vec_specs(layer, d)
    tok = lambda w: pl.BlockSpec((None, tm, w), lambda bi, m: (bi, m, 0))
    edge = lambda n: pl.BlockSpec((None, None, n, LANES), lambda bi, m: (bi, m, 0, 0))
    return pl.pallas_call(
        functools.partial(_fox_in_kernel, width=width, tn=tn),
        grid=(b, s // tm),
        in_specs=[tok(d), ng, modspec(1), modspec(0), _resident((d, n_in), j), _resident((1, LANES))],
        out_specs=[tok(width)] * 4 + [tok(LANES), edge(1), edge(2)],
        out_shape=[jax.ShapeDtypeStruct((b, s, width), BF16)] * 4 + [jax.ShapeDtypeStruct((b, s, LANES), F32)]
                  + [jax.ShapeDtypeStruct((b, s // tm, n, LANES), F32) for n in (1, 2)],
        scratch_shapes=[pltpu.VMEM((1, LANES), F32), pltpu.VMEM((d, 4 * width + LANES), BF16)],
        compiler_params=_params("arbitrary", "arbitrary"),
        name="fox_in_proj",
    )(x, norm_g, mod, mod, w_in, bias)


def _fox_flash_kernel(*refs, bq, hd, seq, bounded):
    if bounded:
        (start_ref, q_ref, k_ref, v_ref, g_ref, cum_ref, qg_ref, kg_ref, bound_ref, y_ref,
         kx_ref, vx_ref, qx_ref, acc_ref) = refs
    else:
        (q_ref, k_ref, v_ref, g_ref, cum_ref, qg_ref, kg_ref, y_ref,
         kx_ref, vx_ref, qx_ref, acc_ref, m_ref) = refs
    pair = pl.program_id(1)
    lane = lax.broadcasted_iota(jnp.int32, (1, LANES), 1)
    first = lane < hd
    sels = (first, jnp.logical_not(first))
    aux0 = (hd, 0)

    def head_norm(x, gain):
        sq = x * x
        s0 = jnp.sum(jnp.where(first, sq, 0.0), axis=-1, keepdims=True)
        s1 = jnp.sum(jnp.where(first, 0.0, sq), axis=-1, keepdims=True)
        ms = jnp.where(first, s0, s1) * (1.0 / hd)
        return x * lax.rsqrt(ms + EPS) * gain

    def lane_move(offset):
        src = lax.broadcasted_iota(jnp.int32, (LANES, LANES), 0)
        dst = lax.broadcasted_iota(jnp.int32, (LANES, LANES), 1)
        blocks = []
        for t in range(3):
            hit = jnp.logical_or(jnp.logical_and(src == 2 * pair, dst == aux0[0] + offset + t),
                                 jnp.logical_and(src == 2 * pair + 1, dst == aux0[1] + offset + t))
            blocks.append(jnp.where(hit, 1.0, 0.0))
        return jnp.concatenate(blocks, axis=0).astype(BF16)

    def bias_lanes(cum, offset, select):
        moved = _dot(jnp.concatenate(_split3(cum), axis=1), select)
        out = []
        for j in range(2):
            one0 = aux0[j] + 3 - offset
            ones = jnp.where(jnp.logical_and(lane >= one0, lane < one0 + 3), 1.0, 0.0)
            out.append(moved + ones)
        return out

    one = jnp.ones((), BF16)
    move_k = lane_move(0)
    move_q = lane_move(3)
    for r0 in range(0, seq, bq):
        kn = head_norm(k_ref[r0:r0 + bq, :].astype(F32), kg_ref[...])
        bias = bias_lanes(cum_ref[r0:r0 + bq, :] * (-LOG2E), 0, move_k)
        v = v_ref[r0:r0 + bq, :]
        for j in range(2):
            kx_ref[j, r0:r0 + bq, :] = jnp.where(sels[j], kn, bias[j]).astype(BF16)
            vx_ref[j, r0:r0 + bq, :] = jnp.where(sels[j], v, one)

    def query_block(qi, carry):
        q0 = pl.multiple_of(qi * bq, bq)
        qrows = pl.ds(q0, bq)
        qn = head_norm(q_ref[qrows, :].astype(F32), qg_ref[...]) * (hd ** -0.5 * LOG2E)
        cum_q = cum_ref[qrows, :] * LOG2E
        if bounded:
            cum_q = cum_q - bound_ref[...]
        bias = bias_lanes(cum_q, 3, move_q)
        for j in range(2):
            qx_ref[j] = jnp.where(sels[j], qn, bias[j]).astype(BF16)
        acc_ref[...] = jnp.zeros_like(acc_ref)
        if not bounded:
            m_ref[...] = jnp.full_like(m_ref, NEG_BIG)
        causal_half = _lower_tri(bq // 2)

        def bounded_blocks(full, diagonal, lead_half=None):
            half = bq // 2
            work = []
            if lead_half is not None:
                rk = pl.ds(pl.multiple_of(lead_half * half, half), half)
                work += [(j, slice(0, bq), rk, None) for j in range(2)]
            for kb in full:
                r0 = pl.multiple_of(kb * bq, bq)
                work += [(j, slice(0, bq), pl.ds(r0, bq), None) for j in range(2)]
            if diagonal:
                for j in range(2):
                    work.append((j, slice(0, half), pl.ds(q0, half), "all"))
                    work.append((j, slice(half, bq), qrows, "right"))
            logits = [lax.dot_general(qx_ref[j, rq, :], kx_ref[j, rk, :], NT_DIMS, preferred_element_type=F32)
                      for j, rq, rk, _ in work]
            add = {}
            for (j, rq, rk, masked), s in zip(work, logits):
                if masked == "all":
                    s = jnp.where(causal_half, s, NEG_BIG)
                elif masked == "right":
                    s = jnp.concatenate([s[:, :half], jnp.where(causal_half, s[:, half:], NEG_BIG)], axis=1)
                pv = _dot(jnp.exp2(s).astype(BF16), vx_ref[j, rk, :])
                key = (j, rq.start, rq.stop)
                add[key] = pv if key not in add else add[key] + pv
            for (j, r0, r1), pv in add.items():
                acc_ref[j, r0:r1, :] += pv

        def step(kb, masked):
            rk = pl.ds(pl.multiple_of(kb * bq, bq), bq)
            logits = [lax.dot_general(qx_ref[j], kx_ref[j, rk, :], NT_DIMS, preferred_element_type=F32)
                      for j in range(2)]
            for j in range(2):
                s = jnp.where(_lower_tri(bq), logits[j], NEG_BIG) if masked else logits[j]
                m_prev = m_ref[j]
                m_new = jnp.maximum(m_prev, jnp.max(s, axis=-1, keepdims=True))
                p = jnp.exp2(s - jnp.concatenate([m_new] * (bq // LANES), axis=1))
                acc_ref[j] = jnp.exp2(m_prev - m_new) * acc_ref[j] + _dot(p.astype(BF16), vx_ref[j, rk, :])
                m_ref[j] = m_new

        if bounded:
            first_half = start_ref[(pl.program_id(0) * pl.num_programs(1) + pair) * (seq // bq) + qi]
            lead = jnp.bitwise_and(first_half, 1) == 1
            start = jnp.right_shift(first_half + 1, 1)
            count = qi - start

            def body(t, carry):
                bounded_blocks([start + 2 * t, start + 2 * t + 1], False)
                return carry

            lax.fori_loop(0, jnp.right_shift(count, 1), body, 0)
            odd = jnp.bitwise_and(count, 1) == 1
            for is_odd in (True, False):
                for has_lead in (True, False):
                    @pl.when(jnp.logical_and(odd == is_odd, lead == has_lead))
                    def _():
                        bounded_blocks([qi - 1] if is_odd else [], True, first_half if has_lead else None)
        else:
            def body(kb, carry):
                step(kb, False)
                return carry

            lax.fori_loop(0, qi, body, 0)
            step(qi, True)
        num = jnp.where(first, acc_ref[0], acc_ref[1])
        den = jnp.where(first, pltpu.roll(acc_ref[0], hd, axis=1), pltpu.roll(acc_ref[1], hd, axis=1))
        y_ref[qrows, :] = (num / den * _silu(g_ref[qrows, :].astype(F32))).astype(BF16)
        return carry

    lax.fori_loop(0, seq // bq, query_block, 0)


def _fox_flash(q, k, v, g, cum, cum_first, cum_last, q_gain, k_gain, bq=512):
    b, s, width = q.shape
    assert cum_first.shape[1] == s // bq and cum_last.shape[2] == 2
    heads = FOX_HEADS
    hd = width // heads
    assert 2 * hd == LANES
    pairs = heads // 2
    qg = jnp.tile(q_gain.astype(F32), 2).reshape(1, LANES)
    kg = jnp.tile(k_gain.astype(F32), 2).reshape(1, LANES)
    nq = s // bq
    full = pl.BlockSpec((None, s, LANES), lambda bi, p, *_: (bi, 0, p))
    row = pl.BlockSpec((1, LANES), lambda bi, p, *_: (0, 0))

    def call(bounded, *extra):
        grid_spec = pltpu.PrefetchScalarGridSpec(
            num_scalar_prefetch=1 if bounded else 0,
            grid=(b, pairs),
            in_specs=[full, full, full, full, pl.BlockSpec((None, s, LANES), lambda bi, p, *_: (bi, 0, 0)),
                      row, row] + ([row] if bounded else []),
            out_specs=full,
            scratch_shapes=[pltpu.VMEM((2, s, LANES), BF16), pltpu.VMEM((2, s, LANES), BF16),
                            pltpu.VMEM((2, bq, LANES), BF16), pltpu.VMEM((2, bq, LANES), F32)]
                           + ([] if bounded else [pltpu.VMEM((2, bq, LANES), F32)]))
        return pl.pallas_call(
            functools.partial(_fox_flash_kernel, bq=bq, hd=hd, seq=s, bounded=bounded),
            grid_spec=grid_spec,
            out_shape=jax.ShapeDtypeStruct((b, s, width), BF16),
            compiler_params=_params("parallel", "parallel"),
            name="fox_flash_bounded" if bounded else "fox_flash_online",
        )(*extra[:1], q, k, v, g, cum, qg, kg, *extra[1:])

    def first_key_blocks():
        at_first = cum_first[:, :, 0, :heads]
        at_last = cum_last[:, :, :, :heads].reshape(b, 2 * nq, heads)
        gap = at_first[:, :, None, :] - at_last[:, None, :, :]
        blocks = jnp.arange(nq, dtype=jnp.int32)
        halves = jnp.arange(2 * nq, dtype=jnp.int32)
        earlier = (halves[None, :] < 2 * blocks[:, None])[None, :, :, None]
        needed = jnp.logical_and(gap * LOG2E > -FOX_SKIP_BITS, earlier)
        first = jnp.where(jnp.any(needed, axis=2), jnp.argmax(needed, axis=2).astype(jnp.int32),
                          2 * blocks[None, :, None])
        first = jnp.min(first.reshape(b, nq, pairs, 2), axis=-1)
        return jnp.transpose(first, (0, 2, 1)).reshape(-1)

    bound = FOX_BOUND_MARGIN * LOG2E * hd ** 0.5 * jnp.max(jnp.abs(qg)) * jnp.max(jnp.abs(kg))
    return lax.cond(bound <= FOX_BOUND_MAX,
                    lambda: call(True, first_key_blocks(), jnp.full((1, LANES), bound, F32)),
                    lambda: call(False))


def _gla_in_kernel(x_ref, ng_ref, sc_ref, sh_ref, wf32_ref, wg2f32_ref, bg_ref,
                   q_ref, k_ref, v_ref, g_ref, la_ref, w_ref, wg2_ref, *, dk, dv, tn):
    _cast_once(wf32_ref, w_ref)

    @pl.when(jnp.logical_and(pl.program_id(0) == 0, pl.program_id(1) == 0))
    def _():
        wg2_ref[...] = jnp.zeros_like(wg2_ref)
        wg2_ref[:wg2f32_ref.shape[0], :] = wg2f32_ref[...].astype(BF16)

    h = _norm_mod(x_ref, ng_ref, sc_ref, sh_ref)
    main = 2 * dk + 2 * dv
    r = _dot(h, w_ref[:, main:])
    _project(h, w_ref, 0, dk, q_ref, tn)
    z = _dot(r.astype(BF16), wg2_ref[...]) + bg_ref[...]
    _project(h, w_ref, dk, dk, k_ref, tn)
    la_ref[...] = _log_sigmoid(z) * (1.0 / GLA_TAU)
    _project(h, w_ref, 2 * dk, dv, v_ref, tn)
    _project(h, w_ref, 2 * dk + dv, dv, g_ref, tn)


def _gla_in_proj(x, norm_g, mod, layer, w_in, w_gate2, j, b_gate, tm=512, tn=512):
    b, s, d = x.shape
    dk, dv, rank = d // 2, d, GLA_RANK
    main = 2 * dk + 2 * dv
    ng, modspec = _layer_vec_specs(layer, d)
    tok = lambda w, : pl.BlockSpec((None, tm, w), lambda bi, m: (bi, m, 0))
    return pl.pallas_call(
        functools.partial(_gla_in_kernel, dk=dk, dv=dv, tn=tn),
        grid=(b, s // tm),
        in_specs=[tok(d), ng, modspec(1), modspec(0), _resident((d, main + rank), j),
                  _resident((rank, dk), j), _resident((1, dk))],
        out_specs=[tok(dk), tok(dk), tok(dv), tok(dv), tok(dk)],
        out_shape=[jax.ShapeDtypeStruct((b, s, w), BF16) for w in (dk, dk, dv, dv)]
                  + [jax.ShapeDtypeStruct((b, s, dk), F32)],
        scratch_shapes=[pltpu.VMEM((d, main + LANES), BF16), pltpu.VMEM((LANES, dk), BF16)],
        compiler_params=_params("arbitrary", "arbitrary"),
        name="gla_in_proj",
    )(x, norm_g, mod, mod, w_in, w_gate2, b_gate.astype(F32).reshape(1, dk))


def _gla_core_kernel(q_ref, k_ref, v_ref, g_ref, la_ref, x_ref, wf32_ref, gate_ref, fg_ref, o_ref,
                     state_ref, y_ref, w_ref, *, heads, hk, hv, nchunk, final):
    _cast_once(wf32_ref, w_ref)

    @pl.when(pl.program_id(1) == 0)
    def _():
        state_ref[...] = jnp.zeros_like(state_ref)

    tile = nchunk * CHUNK
    row = lax.broadcasted_iota(jnp.int32, (tile, tile), 0)
    col = lax.broadcasted_iota(jnp.int32, (tile, tile), 1)
    same = (row // CHUNK) == (col // CHUNK)
    causal = row >= col
    tri = jnp.where(jnp.logical_and(same, causal), 1.0, 0.0).astype(BF16)
    scale = hk ** -0.5
    dk = heads * hk
    chunk_rows = [slice(c * CHUNK, (c + 1) * CHUNK) for c in range(nchunk)]

    def run_tile(rows):
        cb = _tri_cumsum(tri, la_ref[rows, :], pieces=2)
        lasts = [cb[r.stop - 1:r.stop, :] for r in chunk_rows]
        cb_last = jnp.concatenate([jnp.broadcast_to(last, (CHUNK, dk)) for last in lasts], axis=0)
        eb = jnp.exp(cb)
        enb = jnp.exp(-cb)
        qf = q_ref[rows, :].astype(F32) * scale
        kf = k_ref[rows, :].astype(F32)
        q_e = (qf * eb).astype(BF16)
        q_n = (qf * enb).astype(BF16)
        k_n = (kf * enb).astype(BF16)
        k_e = (kf * eb).astype(BF16)
        k_in = (kf * jnp.exp(cb_last - cb)).astype(BF16)
        for h in range(heads):
            ks = slice(h * hk, (h + 1) * hk)
            vs = slice(h * hv, (h + 1) * hv)
            v = v_ref[rows, vs]
            a_causal = lax.dot_general(q_e[:, ks], k_n[:, ks], NT_DIMS, preferred_element_type=F32)
            a_anti = lax.dot_general(q_n[:, ks], k_e[:, ks], NT_DIMS, preferred_element_type=F32)
            attn = jnp.where(same, jnp.where(causal, a_causal, a_anti), 0.0).astype(BF16)
            updates = [lax.dot_general(k_in[r, ks], v[r, :], TN_DIMS, preferred_element_type=F32)
                       for r in chunk_rows]
            o = _dot(attn, v)
            state = state_ref[h]
            carried = []
            for c in range(nchunk):
                carried.append(state.astype(BF16))
                decay_col = jnp.broadcast_to(jnp.exp(lasts[c][:, ks]), (hk, hk)).T
                state = state * jnp.concatenate([decay_col] * (hv // hk), axis=1) + updates[c]
            state_ref[h] = state
            o = o + jnp.concatenate([_dot(q_e[r, ks], carried[c]) for c, r in enumerate(chunk_rows)], axis=0)
            ms = jnp.mean(o * o, axis=-1, keepdims=True)
            gate = _silu(g_ref[rows, vs].astype(F32))
            y_ref[rows, vs] = (o * lax.rsqrt(ms + EPS) * gate).astype(BF16)

    for t0 in range(0, q_ref.shape[0], tile):
        run_tile(slice(t0, t0 + tile))
    _residual_out(y_ref, w_ref, x_ref, gate_ref, fg_ref, o_ref, final)


def _gla_core(q, k, v, g, la, x, w_out, j, mod, layer, final_g, final, tile=256, tl=1024):
    b, s, dk = q.shape
    dv = v.shape[-1]
    d = x.shape[-1]
    heads = GLA_HEADS
    hk, hv = dk // heads, dv // heads
    _, modspec = _layer_vec_specs(layer, d)
    tok = lambda w: pl.BlockSpec((None, tl, w), lambda bi, l: (bi, l, 0))
    return pl.pallas_call(
        functools.partial(_gla_core_kernel, heads=heads, hk=hk, hv=hv, nchunk=tile // CHUNK, final=final),
        grid=(b, s // tl),
        in_specs=[tok(dk), tok(dk), tok(dv), tok(dv), tok(dk), tok(d), _resident((dv, d), j), modspec(2),
                  _resident((1, d))],
        out_specs=tok(d),
        out_shape=jax.ShapeDtypeStruct((b, s, d), F32),
        scratch_shapes=[pltpu.VMEM((heads, hk, hv), F32), pltpu.VMEM((tl, dv), BF16), pltpu.VMEM((dv, d), BF16)],
        compiler_params=_params("arbitrary", "arbitrary"),
        name="gla_core_out",
    )(q, k, v, g, la, x, w_out, mod, final_g)


def _out_kernel(y_ref, wf32_ref, x_ref, gate_ref, fg_ref, o_ref, w_ref, *, final):
    _cast_once(wf32_ref, w_ref)
    _residual_out(y_ref, w_ref, x_ref, gate_ref, fg_ref, o_ref, final)


def _out_proj(y, w_out, j, x, mod, layer, final_g, final, tm=1024):
    b, s, d = x.shape
    kdim = y.shape[-1]
    _, modspec = _layer_vec_specs(layer, d)
    tok = lambda w: pl.BlockSpec((None, tm, w), lambda bi, m: (bi, m, 0))
    return pl.pallas_call(
        functools.partial(_out_kernel, final=final),
        grid=(b, s // tm),
        in_specs=[tok(kdim), _resident((kdim, d), j), tok(d), modspec(2), _resident((1, d))],
        out_specs=tok(d),
        out_shape=jax.ShapeDtypeStruct((b, s, d), F32),
        scratch_shapes=[pltpu.VMEM((kdim, d), BF16)],
        compiler_params=_params("arbitrary", "arbitrary"),
        name="out_proj_residual",
    )(y, w_out, x, mod, final_g)


def kernel(x, c, positions, mod_w, mod_b, norm_g, ret_w_in, ret_w_out, fox_w_in, fox_b_f, fox_q_gain,
           fox_k_gain, fox_w_out, gla_w_in, gla_w_gate2, gla_b_gate, gla_w_out, final_g):
    depth, d, _ = mod_w.shape
    b = x.shape[0]
    mod = _modulation(c, mod_w, mod_b).reshape(depth, b, 3, 1, d)
    ng = norm_g.astype(F32).reshape(depth, 1, d)
    fg = final_g.astype(F32).reshape(1, d)
    cos, sin = _rope_tables(positions, d // RET_HEADS // 2)
    ret_w_in_bf = ret_w_in.astype(BF16)
    for i in range(depth):
        j = i // N_MIXERS
        kind = i % N_MIXERS
        final = i == depth - 1
        if kind == 0:
            q, k, v, g = _ret_in_proj(x, cos, sin, ng, mod, i, ret_w_in_bf, j)
            x = _ret_core(q, k, v, g, x, ret_w_out, j, mod, i, fg, final)
        elif kind == 1:
            q, k, v, g, cum, cum_first, cum_last = _fox_in_proj(x, ng, mod, i, fox_w_in, j, fox_b_f[j])
            y = _fox_flash(q, k, v, g, cum, cum_first, cum_last, fox_q_gain[j], fox_k_gain[j])
            x = _out_proj(y, fox_w_out, j, x, mod, i, fg, final)
        else:
            q, k, v, g, la = _gla_in_proj(x, ng, mod, i, gla_w_in, gla_w_gate2, j, gla_b_gate[j])
            x = _gla_core(q, k, v, g, la, x, gla_w_out, j, mod, i, fg, final)
    return x
```
